```python
import math
import jax, jax.numpy as jnp
from jax import lax
import numpy as np

D_MODEL = 4096
BATCH = 1
SEQ = 8192
DEPTH = 2

HEAD_DIM = 128
MIX_WIDTH = D_MODEL
MEM_WIDTH = MIX_WIDTH // 4
SELF_WIDTH = MIX_WIDTH - MEM_WIDTH
MOBA_HEADS = SELF_WIDTH // HEAD_DIM
DIFF_HEADS = SELF_WIDTH // (2 * HEAD_DIM)
MEM_HEADS = 4
MEM_HEAD_DIM = MEM_WIDTH // MEM_HEADS
MEM_LEN = 256
PROJ_WIDTH = 3 * SELF_WIDTH + MEM_WIDTH
ROT_DIM = HEAD_DIM // 4
ROPE_THETA = 500000.0
MOBA_BLOCK = 256
MOBA_TOPK = 3
MOBA_Q_CHUNK = 32
ATTN_Q_BLOCK = 128
_FF_RAW = (8 * D_MODEL + 2) // 3
D_FF = ((_FF_RAW + 255) // 256) * 256
N_DIFF = DEPTH // 2
NORM_EPS = 1e-6
SUBLN_EPS = 1e-5

kernel_name = 'hybrid_moba_diffattn_memory_block'

f32 = jnp.float32


def _rms(x, g, eps=NORM_EPS):
    xf = x.astype(f32)
    y = xf * lax.rsqrt(jnp.mean(xf * xf, axis=-1, keepdims=True) + eps)
    return (y * g.astype(f32)).astype(x.dtype)


def _rope_tables(positions):
    inv = ROPE_THETA ** (-jnp.arange(0, ROT_DIM, 2, dtype=f32) / ROT_DIM)
    ang = positions.astype(f32)[..., None] * inv
    return jnp.cos(ang), jnp.sin(ang)


def _partial_rope(x, cos, sin):
    c = cos[:, :, None, :].astype(x.dtype)
    s = sin[:, :, None, :].astype(x.dtype)
    half = ROT_DIM // 2
    x1 = x[..., :half]
    x2 = x[..., half:ROT_DIM]
    return jnp.concatenate([x1 * c - x2 * s, x2 * c + x1 * s, x[..., ROT_DIM:]], axis=-1)


def _moba_attention(q, k, v):
    B, S, H, D = q.shape
    n_blk = -(-S // MOBA_BLOCK)
    pad = n_blk * MOBA_BLOCK - S
    n_sel = min(MOBA_TOPK, max(n_blk - 1, 1))
    scale = HEAD_DIM ** -0.5
    widths = ((0, 0), (0, pad), (0, 0), (0, 0))
    kb = jnp.pad(k, widths).reshape(B, n_blk, MOBA_BLOCK, H, D).transpose(0, 3, 1, 2, 4)
    vb = jnp.pad(v, widths).reshape(B, n_blk, MOBA_BLOCK, H, D).transpose(0, 3, 1, 2, 4)
    counts = jnp.clip(S - jnp.arange(n_blk) * MOBA_BLOCK, 1, MOBA_BLOCK).astype(f32)
    k_mean = (kb.astype(f32).sum(axis=3) / counts[None, None, :, None]).astype(q.dtype)
    n_chunk = S // MOBA_Q_CHUNK
    q_chunks = jnp.moveaxis(q.transpose(0, 2, 1, 3).reshape(B, H, n_chunk, MOBA_Q_CHUNK, D), 2, 0)
    b_idx = jnp.arange(B)[:, None, None, None]
    h_idx = jnp.arange(H)[None, :, None, None]
    blk_ids = jnp.arange(n_blk)

    def chunk(args):
        qc, c = args
        start = c * MOBA_Q_CHUNK
        b0 = start // MOBA_BLOCK
        t = start + jnp.arange(MOBA_Q_CHUNK)
        gate = jnp.einsum('bhqd,bhnd->bhqn', qc, k_mean).astype(f32)
        gate = jnp.where(blk_ids < b0, gate, -jnp.inf)
        _, sel = lax.top_k(gate, n_sel)
        valid = sel < b0
        k_sel = kb[b_idx, h_idx, sel]
        v_sel = vb[b_idx, h_idx, sel]
        s_sel = jnp.einsum('bhqd,bhqnkd->bhqnk', qc, k_sel).astype(f32) * scale
        s_sel = jnp.where(valid[..., None], s_sel, -jnp.inf).reshape(B, H, MOBA_Q_CHUNK, n_sel * MOBA_BLOCK)
        k_own = lax.dynamic_index_in_dim(kb, b0, axis=2, keepdims=False)
        v_own = lax.dynamic_index_in_dim(vb, b0, axis=2, keepdims=False)
        s_own = jnp.einsum('bhqd,bhkd->bhqk', qc, k_own).astype(f32) * scale
        key_pos = b0 * MOBA_BLOCK + jnp.arange(MOBA_BLOCK)
        s_own = jnp.where(key_pos[None, :] <= t[:, None], s_own, -jnp.inf)
        p = jax.nn.softmax(jnp.concatenate([s_sel, s_own], axis=-1), axis=-1).astype(v.dtype)
        p_sel = p[..., :n_sel * MOBA_BLOCK].reshape(B, H, MOBA_Q_CHUNK, n_sel, MOBA_BLOCK)
        p_own = p[..., n_sel * MOBA_BLOCK:]
        return (jnp.einsum('bhqnk,bhqnkd->bhqd', p_sel, v_sel)
                + jnp.einsum('bhqk,bhkd->bhqd', p_own, v_own))

    out = lax.map(chunk, (q_chunks, jnp.arange(n_chunk)))
    out = jnp.moveaxis(out, 0, 2).reshape(B, H, S, D).transpose(0, 2, 1, 3)
    return out.reshape(B, S, H * D)


def _diff_attention(q, k, v, lam):
    B, S, H, _, D = q.shape
    scale = D ** -0.5
    n_blk = S // ATTN_Q_BLOCK
    q_blocks = jnp.moveaxis(q.reshape(B, n_blk, ATTN_Q_BLOCK, H, 2, D), 1, 0)
    key_pos = jnp.arange(S)

    def block(args):
        qb, c = args
        t = c * ATTN_Q_BLOCK + jnp.arange(ATTN_Q_BLOCK)
        s = jnp.einsum('bqhcd,bkhcd->bhcqk', qb, k).astype(f32) * scale
        s = jnp.where(key_pos[None, :] <= t[:, None], s, -jnp.inf)
        p = jax.nn.softmax(s, axis=-1)
        a = p[:, :, 0] - lam * p[:, :, 1]
        return jnp.einsum('bhqk,bkhe->bqhe', a.astype(v.dtype), v)

    out = lax.map(block, (q_blocks, jnp.arange(n_blk)))
    return jnp.moveaxis(out, 0, 1).reshape(B, S, H, 2 * D)


def _mem_attention(qm, km, vm):
    B, S = qm.shape[:2]
    s = jnp.einsum('bqhd,bmhd->bhqm', qm, km).astype(f32) * (MEM_HEAD_DIM ** -0.5)
    p = jax.nn.softmax(s, axis=-1).astype(vm.dtype)
    return jnp.einsum('bhqm,bmhd->bqhd', p, vm).reshape(B, S, MEM_WIDTH)


def setup_inputs(seed: int = 0) -> dict:
    key = jax.random.key(seed)
    ks = jax.random.split(key, 24)

    def nrm(k, shape, scale):
        return jax.random.normal(k, shape, f32) * scale

    def gain(k, shape):
        return 1.0 + 0.02 * jax.random.normal(k, shape, f32)

    return {
        'x': nrm(ks[0], (BATCH, SEQ, D_MODEL), 1.0),
        'mem': nrm(ks[1], (BATCH, MEM_LEN, D_MODEL), 1.0),
        'positions': jnp.broadcast_to(jnp.arange(SEQ, dtype=jnp.int32)[None, :], (BATCH, SEQ)),
        'g_attn_norm': gain(ks[2], (DEPTH, D_MODEL)),
        'w_in': nrm(ks[3], (DEPTH, D_MODEL, PROJ_WIDTH), D_MODEL ** -0.5),
        'w_out': nrm(ks[4], (DEPTH, MIX_WIDTH, D_MODEL), MIX_WIDTH ** -0.5),
        'g_qnorm': gain(ks[5], (DEPTH, HEAD_DIM)),
        'g_knorm': gain(ks[6], (DEPTH, HEAD_DIM)),
        'g_mem_qnorm': gain(ks[7], (DEPTH, MEM_HEAD_DIM)),
        'g_mem_knorm': gain(ks[8], (DEPTH, MEM_HEAD_DIM)),
        'g_mem_norm': gain(ks[9], (D_MODEL,)),
        'w_mem_kv': nrm(ks[10], (D_MODEL, 2 * MEM_WIDTH), D_MODEL ** -0.5),
        'lambda_q1': nrm(ks[11], (N_DIFF, HEAD_DIM), 0.1),
        'lambda_k1': nrm(ks[12], (N_DIFF, HEAD_DIM), 0.1),
        'lambda_q2': nrm(ks[13], (N_DIFF, HEAD_DIM), 0.1),
        'lambda_k2': nrm(ks[14], (N_DIFF, HEAD_DIM), 0.1),
        'g_subln': gain(ks[15], (N_DIFF, 2 * HEAD_DIM)),
        'g_ffn_norm': gain(ks[16], (DEPTH, D_MODEL)),
        'w_gate': nrm(ks[17], (DEPTH, D_MODEL, D_FF), D_MODEL ** -0.5),
        'w_up': nrm(ks[18], (DEPTH, D_MODEL, D_FF), D_MODEL ** -0.5),
        'w_down': nrm(ks[19], (DEPTH, D_FF, D_MODEL), D_FF ** -0.5),
    }


def reference(x, mem, positions, g_attn_norm, w_in, w_out, g_qnorm, g_knorm,
              g_mem_qnorm, g_mem_knorm, g_mem_norm, w_mem_kv, lambda_q1, lambda_k1,
              lambda_q2, lambda_k2, g_subln, g_ffn_norm, w_gate, w_up, w_down):
    B, S, _ = x.shape
    M = mem.shape[1]
    cos, sin = _rope_tables(positions)
    mkv = _rms(mem, g_mem_norm) @ w_mem_kv
    mk_raw = mkv[..., :MEM_WIDTH].reshape(B, M, MEM_HEADS, MEM_HEAD_DIM)
    mv = mkv[..., MEM_WIDTH:].reshape(B, M, MEM_HEADS, MEM_HEAD_DIM)

    for i in range(DEPTH):
        h = _rms(x, g_attn_norm[i])
        proj = h @ w_in[i]
        qs = proj[..., :SELF_WIDTH]
        ks_ = proj[..., SELF_WIDTH:2 * SELF_WIDTH]
        vs = proj[..., 2 * SELF_WIDTH:3 * SELF_WIDTH]
        qm = proj[..., 3 * SELF_WIDTH:]
        if i % 2 == 0:
            q = _partial_rope(_rms(qs.reshape(B, S, MOBA_HEADS, HEAD_DIM), g_qnorm[i]), cos, sin)
            k = _partial_rope(_rms(ks_.reshape(B, S, MOBA_HEADS, HEAD_DIM), g_knorm[i]), cos, sin)
            v = vs.reshape(B, S, MOBA_HEADS, HEAD_DIM)
            self_out = _moba_attention(q, k, v)
        else:
            j = i // 2
            q = _partial_rope(_rms(qs.reshape(B, S, 2 * DIFF_HEADS, HEAD_DIM), g_qnorm[i]), cos, sin)
            k = _partial_rope(_rms(ks_.reshape(B, S, 2 * DIFF_HEADS, HEAD_DIM), g_knorm[i]), cos, sin)
            q = q.reshape(B, S, DIFF_HEADS, 2, HEAD_DIM)
            k = k.reshape(B, S, DIFF_HEADS, 2, HEAD_DIM)
            v = vs.reshape(B, S, DIFF_HEADS, 2 * HEAD_DIM)
            lam_init = 0.8 - 0.6 * math.exp(-0.3 * i)
            lam = (jnp.exp(jnp.sum(lambda_q1[j].astype(f32) * lambda_k1[j].astype(f32)))
                   - jnp.exp(jnp.sum(lambda_q2[j].astype(f32) * lambda_k2[j].astype(f32)))
                   + lam_init)
            o = _diff_attention(q, k, v, lam)
            o = _rms(o, g_subln[j], SUBLN_EPS) * (1.0 - lam_init)
            self_out = o.reshape(B, S, SELF_WIDTH)
        qmh = _rms(qm.reshape(B, S, MEM_HEADS, MEM_HEAD_DIM), g_mem_qnorm[i])
        kmh = _rms(mk_raw, g_mem_knorm[i])
        mem_out = _mem_attention(qmh, kmh, mv)
        x = x + jnp.concatenate([self_out, mem_out], axis=-1) @ w_out[i]
        f = _rms(x, g_ffn_norm[i])
        x = x + (jax.nn.silu(f @ w_gate[i]) * (f @ w_up[i])) @ w_down[i]
    return x
```

```python
import functools
import math

import jax
import jax.numpy as jnp
from jax import lax
from jax.experimental import pallas as pl
from jax.experimental.pallas import tpu as pltpu

f32 = jnp.float32
bf16 = jnp.bfloat16

HEAD_DIM = 128
MEM_HEADS = 4
ROT_DIM = HEAD_DIM // 4
ROPE_THETA = 500000.0
MOBA_BLOCK = 256
MOBA_TOPK = 3
NORM_EPS = 1e-6
SUBLN_EPS = 1e-5
MASK_VALUE = -1e30

LANES = 128
ATTN_TILE = 256
VMEM_LIMIT = 56 * 1024 * 1024


def _tile(n, target, align, col0=0):
    t = (min(target, n) // align) * align
    while n % t or col0 % t:
        t -= align
    return t


def _params(n_grid):
    return pltpu.CompilerParams(
        dimension_semantics=("arbitrary",) * n_grid, vmem_limit_bytes=VMEM_LIMIT)


def _rmsnorm_kernel(x_ref, g_ref, o_ref, *, eps):
    x = x_ref[...]
    ms = jnp.mean(x * x, axis=-1, keepdims=True)
    o_ref[...] = ((x * lax.rsqrt(ms + eps)) * g_ref[...]).astype(o_ref.dtype)


def _rmsnorm(x, g, *, eps=NORM_EPS, tm=256):
    m, d = x.shape
    tm = min(tm, m)
    return pl.pallas_call(
        functools.partial(_rmsnorm_kernel, eps=eps),
        grid=(m // tm,),
        in_specs=[pl.BlockSpec((tm, d), lambda i: (i, 0)),
                  pl.BlockSpec((1, d), lambda i: (0, 0))],
        out_specs=pl.BlockSpec((tm, d), lambda i: (i, 0)),
        out_shape=jax.ShapeDtypeStruct((m, d), bf16),
        compiler_params=_params(1),
        name="rmsnorm",
    )(x, g.reshape(1, d))


def _rope_tables_kernel(pos_ref, inv_ref, c_ref, sa_ref, sb_ref):
    ang = pos_ref[...].astype(f32) * inv_ref[...]
    lane = lax.broadcasted_iota(jnp.int32, ang.shape, 1)
    c = jnp.cos(ang)
    s = jnp.sin(ang)
    half = ROT_DIM // 2
    c_ref[...] = jnp.where(lane < ROT_DIM, c, 1.0)
    sa_ref[...] = jnp.where((lane >= half) & (lane < ROT_DIM), s, 0.0)
    sb_ref[...] = jnp.where(lane < half, -s, 0.0)


def _rope_tables(positions, *, tm=1024):
    s = positions.shape[0]
    tm = min(tm, s)
    inv = ROPE_THETA ** (-jnp.arange(0, ROT_DIM, 2, dtype=f32) / ROT_DIM)
    inv_lane = jnp.concatenate(
        [inv, inv, jnp.zeros((HEAD_DIM - ROT_DIM,), f32)]).reshape(1, HEAD_DIM)
    spec = pl.BlockSpec((tm, HEAD_DIM), lambda i: (i, 0))
    shape = jax.ShapeDtypeStruct((s, HEAD_DIM), f32)
    return pl.pallas_call(
        _rope_tables_kernel,
        grid=(s // tm,),
        in_specs=[pl.BlockSpec((tm, 1), lambda i: (i, 0)),
                  pl.BlockSpec((1, HEAD_DIM), lambda i: (0, 0))],
        out_specs=[spec, spec, spec],
        out_shape=[shape, shape, shape],
        compiler_params=_params(1),
        name="rope_tables",
    )(positions.reshape(s, 1), inv_lane)


def _proj_heads_kernel(a_ref, w_ref, g1_ref, g2_ref, c_ref, sa_ref, sb_ref, o_ref,
                       *, head, n_first, rope):
    acc = jnp.dot(a_ref[...], w_ref[...], preferred_element_type=f32)
    g = jnp.where(pl.program_id(1) < n_first, g1_ref[...], g2_ref[...])
    half = ROT_DIM // 2
    for h in range(acc.shape[1] // head):
        t = acc[:, h * head:(h + 1) * head]
        ms = jnp.mean(t * t, axis=-1, keepdims=True)
        y = (t * lax.rsqrt(ms + NORM_EPS)) * g
        if rope:
            y = (y * c_ref[...] + pltpu.roll(y, half, axis=1) * sa_ref[...]
                 + pltpu.roll(y, head - half, axis=1) * sb_ref[...])
        o_ref[:, h * head:(h + 1) * head] = y.astype(o_ref.dtype)


def _proj_heads(a, w, col0, n, g1, g2, n_first_cols, tables, *, head, rope, tm=1024, tn=1024):
    m, k = a.shape
    tm = min(tm, m)
    tn = _tile(n, tn, head, math.gcd(col0, n_first_cols))
    c, sa, sb = tables
    off = col0 // tn
    tab_spec = pl.BlockSpec((tm, HEAD_DIM), lambda i, j: (i, 0))
    g_spec = pl.BlockSpec((1, head), lambda i, j: (0, 0))
    return pl.pallas_call(
        functools.partial(_proj_heads_kernel, head=head, n_first=n_first_cols // tn, rope=rope),
        grid=(m // tm, n // tn),
        in_specs=[pl.BlockSpec((tm, k), lambda i, j: (i, 0)),
                  pl.BlockSpec((k, tn), lambda i, j: (0, j + off)),
                  g_spec, g_spec, tab_spec, tab_spec, tab_spec],
        out_specs=pl.BlockSpec((tm, tn), lambda i, j: (i, j)),
        out_shape=jax.ShapeDtypeStruct((m, n), bf16),
        compiler_params=_params(2),
        name="proj_heads",
    )(a, w, g1.reshape(1, head), g2.reshape(1, head), c, sa, sb)


def _matmul_kernel(a_ref, w_ref, o_ref):
    o_ref[...] = jnp.dot(a_ref[...], w_ref[...], preferred_element_type=f32).astype(o_ref.dtype)


def _matmul(a, w, col0, n, out_dtype, *, tm=1024, tn=1024):
    m, k = a.shape
    tm = min(tm, m)
    tn = _tile(n, tn, LANES, col0)
    off = col0 // tn
    return pl.pallas_call(
        _matmul_kernel,
        grid=(m // tm, n // tn),
        in_specs=[pl.BlockSpec((tm, k), lambda i, j: (i, 0)),
                  pl.BlockSpec((k, tn), lambda i, j: (0, j + off))],
        out_specs=pl.BlockSpec((tm, tn), lambda i, j: (i, j)),
        out_shape=jax.ShapeDtypeStruct((m, n), out_dtype),
        compiler_params=_params(2),
        name="matmul",
    )(a, w)


def _outproj_kernel(a1_ref, a2_ref, w1_ref, w2_ref, r_ref, o_ref):
    acc = jnp.dot(a1_ref[...], w1_ref[...], preferred_element_type=f32)
    acc = acc + jnp.dot(a2_ref[...], w2_ref[...], preferred_element_type=f32)
    o_ref[...] = r_ref[...] + acc


def _outproj(a1, a2, w, resid, *, tm=1024, tn=512):
    m, k1 = a1.shape
    k2 = a2.shape[1]
    n = w.shape[1]
    tm = min(tm, m)
    tn = _tile(n, tn, LANES)
    assert k1 % k2 == 0
    return pl.pallas_call(
        _outproj_kernel,
        grid=(m // tm, n // tn),
        in_specs=[pl.BlockSpec((tm, k1), lambda i, j: (i, 0)),
                  pl.BlockSpec((tm, k2), lambda i, j: (i, 0)),
                  pl.BlockSpec((k1, tn), lambda i, j: (0, j)),
                  pl.BlockSpec((k2, tn), lambda i, j: (k1 // k2, j)),
                  pl.BlockSpec((tm, tn), lambda i, j: (i, j))],
        out_specs=pl.BlockSpec((tm, tn), lambda i, j: (i, j)),
        out_shape=jax.ShapeDtypeStruct((m, n), f32),
        compiler_params=_params(2),
        name="outproj",
    )(a1, a2, w, w, resid)


def _ffn_up_kernel(a_ref, wg_ref, wu_ref, o_ref):
    a = a_ref[...]
    g = jnp.dot(a, wg_ref[...], preferred_element_type=f32)
    u = jnp.dot(a, wu_ref[...], preferred_element_type=f32)
    o_ref[...] = ((g * jax.nn.sigmoid(g)) * u).astype(o_ref.dtype)


def _ffn_up(a, wg, wu, *, tm=1024, tn=256):
    m, k = a.shape
    n = wg.shape[1]
    tm = min(tm, m)
    tn = _tile(n, tn, LANES)
    assert n % tn == 0
    w_spec = pl.BlockSpec((k, tn), lambda i, j: (0, j))
    return pl.pallas_call(
        _ffn_up_kernel,
        grid=(m // tm, n // tn),
        in_specs=[pl.BlockSpec((tm, k), lambda i, j: (i, 0)), w_spec, w_spec],
        out_specs=pl.BlockSpec((tm, tn), lambda i, j: (i, j)),
        out_shape=jax.ShapeDtypeStruct((m, n), bf16),
        compiler_params=_params(2),
        name="ffn_up",
    )(a, wg, wu)


def _ffn_down_kernel(a_ref, w_ref, r_ref, o_ref):
    o_ref[...] = r_ref[...] + jnp.dot(a_ref[...], w_ref[...], preferred_element_type=f32)


def _ffn_down(a, w, resid, *, tm=512, tn=512):
    m, k = a.shape
    n = w.shape[1]
    tm = min(tm, m)
    tn = _tile(n, tn, LANES)
    return pl.pallas_call(
        _ffn_down_kernel,
        grid=(m // tm, n // tn),
        in_specs=[pl.BlockSpec((tm, k), lambda i, j: (i, 0)),
                  pl.BlockSpec((k, tn), lambda i, j: (0, j)),
                  pl.BlockSpec((tm, tn), lambda i, j: (i, j))],
        out_specs=pl.BlockSpec((tm, tn), lambda i, j: (i, j)),
        out_shape=jax.ShapeDtypeStruct((m, n), f32),
        compiler_params=_params(2),
        name="ffn_down",
    )(a, w, resid)


def _transpose_bf16(x):
    return x.astype(f32).T.astype(bf16)


def _flash_stream(q_t, k_ref, c0, d, v_t_ref, i, bias_ref, *, scale):
    t = ATTN_TILE
    row0 = pl.multiple_of(i * t, t)
    s = jnp.dot(k_ref[pl.ds(row0, t), c0:c0 + d], q_t, preferred_element_type=f32) * scale
    key = lax.broadcasted_iota(jnp.int32, (t, t), 0)
    qry = lax.broadcasted_iota(jnp.int32, (t, t), 1)
    s = jnp.where(key <= qry, s, MASK_VALUE)
    m = jnp.max(s, axis=0, keepdims=True)
    p = jnp.exp(s - m)
    l = jnp.sum(p, axis=0, keepdims=True)
    acc = jnp.dot(v_t_ref[i], p.astype(bf16), preferred_element_type=f32)

    def body(n, carry):
        m, l, acc = carry
        r0 = pl.multiple_of(n * t, t)
        s = jnp.dot(k_ref[pl.ds(r0, t), c0:c0 + d], q_t, preferred_element_type=f32) * scale
        if bias_ref is not None:
            s = s + bias_ref[pl.ds(n, 1), :]
        m_new = jnp.maximum(m, jnp.max(s, axis=0, keepdims=True))
        alpha = jnp.exp(m - m_new)
        p = jnp.exp(s - m_new)
        l = alpha * l + jnp.sum(p, axis=0, keepdims=True)
        acc = alpha * acc + jnp.dot(v_t_ref[n], p.astype(bf16), preferred_element_type=f32)
        return m_new, l, acc

    _, l, acc = lax.fori_loop(0, i, body, (m, l, acc))
    return acc, l


def _fill_v_t(v_ref, v_t_ref):
    t = ATTN_TILE
    for n in range(v_t_ref.shape[0]):
        v_t_ref[n] = _transpose_bf16(v_ref[n * t:(n + 1) * t, :])


def _moba_kernel(q_ref, k_ref, v_ref, o_ref, v_t_ref, kmean_ref, bias_ref, *, n_sel):
    i = pl.program_id(1)
    t = ATTN_TILE
    nb = v_t_ref.shape[0]

    @pl.when(i == 0)
    def _():
        _fill_v_t(v_ref, v_t_ref)
        for n in range(nb):
            kb = k_ref[n * t:(n + 1) * t, :].astype(f32)
            kmean_ref[n:n + 1, :] = jnp.sum(kb, axis=0, keepdims=True) / float(t)

    q_t = _transpose_bf16(q_ref[...])
    gate = jnp.dot(kmean_ref[...].astype(bf16), q_t, preferred_element_type=f32)
    blk = lax.broadcasted_iota(jnp.int32, gate.shape, 0)
    past = blk < i
    g = jnp.where(past, gate, -jnp.inf)
    sel = jnp.zeros(gate.shape, jnp.bool_)
    for _ in range(n_sel):
        mx = jnp.max(g, axis=0, keepdims=True)
        idx = jnp.min(jnp.where(g == mx, blk, nb), axis=0, keepdims=True)
        pick = blk == idx
        sel = sel | pick
        g = jnp.where(pick, -jnp.inf, g)
    bias_ref[...] = jnp.where(sel & past, 0.0, MASK_VALUE)

    acc, l = _flash_stream(q_t, k_ref, 0, HEAD_DIM, v_t_ref, i, bias_ref, scale=HEAD_DIM ** -0.5)
    o_ref[...] = (acc * (1.0 / l)).T.astype(o_ref.dtype)


def _moba_attention(qk, v, n_heads):
    s = qk.shape[0]
    t = ATTN_TILE
    assert t == MOBA_BLOCK and s % t == 0
    nb = s // t
    n_sel = min(MOBA_TOPK, max(nb - 1, 1))
    d = HEAD_DIM
    return pl.pallas_call(
        functools.partial(_moba_kernel, n_sel=n_sel),
        grid=(n_heads, nb),
        in_specs=[pl.BlockSpec((t, d), lambda h, i: (i, h)),
                  pl.BlockSpec((s, d), lambda h, i: (0, n_heads + h)),
                  pl.BlockSpec((s, d), lambda h, i: (0, h))],
        out_specs=pl.BlockSpec((t, d), lambda h, i: (i, h)),
        out_shape=jax.ShapeDtypeStruct((s, n_heads * d), bf16),
        scratch_shapes=[pltpu.VMEM((nb, d, t), bf16),
                        pltpu.VMEM((nb, d), f32),
                        pltpu.VMEM((nb, t), f32)],
        compiler_params=_params(2),
        name="moba_attention",
    )(qk, qk, v)


def _diff_kernel(q_ref, k_ref, v_ref, lq1_ref, lk1_ref, lq2_ref, lk2_ref, g_ref, o_ref, v_t_ref,
                 *, lam_init):
    i = pl.program_id(1)
    d = HEAD_DIM

    @pl.when(i == 0)
    def _():
        _fill_v_t(v_ref, v_t_ref)

    lam = (jnp.exp(jnp.sum(lq1_ref[...] * lk1_ref[...], axis=-1, keepdims=True))
           - jnp.exp(jnp.sum(lq2_ref[...] * lk2_ref[...], axis=-1, keepdims=True))
           + lam_init)
    q_t = _transpose_bf16(q_ref[...])
    scale = d ** -0.5
    acc1, l1 = _flash_stream(q_t[:d], k_ref, 0, d, v_t_ref, i, None, scale=scale)
    acc2, l2 = _flash_stream(q_t[d:], k_ref, d, d, v_t_ref, i, None, scale=scale)
    o = acc1 * (1.0 / l1) - lam * (acc2 * (1.0 / l2))
    ms = jnp.mean(o * o, axis=0, keepdims=True)
    y = (o * lax.rsqrt(ms + SUBLN_EPS)).T * g_ref[...]
    o_ref[...] = (y * (1.0 - lam_init)).astype(o_ref.dtype)


def _diff_attention(qk, v, n_heads, lq1, lk1, lq2, lk2, g_subln, lam_init):
    s = qk.shape[0]
    t = ATTN_TILE
    assert s % t == 0
    nb = s // t
    d2 = 2 * HEAD_DIM
    vec = pl.BlockSpec((1, HEAD_DIM), lambda h, i: (0, 0))
    return pl.pallas_call(
        functools.partial(_diff_kernel, lam_init=lam_init),
        grid=(n_heads, nb),
        in_specs=[pl.BlockSpec((t, d2), lambda h, i: (i, h)),
                  pl.BlockSpec((s, d2), lambda h, i: (0, n_heads + h)),
                  pl.BlockSpec((s, d2), lambda h, i: (0, h)),
                  vec, vec, vec, vec,
                  pl.BlockSpec((1, d2), lambda h, i: (0, 0))],
        out_specs=pl.BlockSpec((t, d2), lambda h, i: (i, h)),
        out_shape=jax.ShapeDtypeStruct((s, n_heads * d2), bf16),
        scratch_shapes=[pltpu.VMEM((nb, d2, t), bf16)],
        compiler_params=_params(2),
        name="diff_attention",
    )(qk, qk, v, lq1.reshape(1, -1), lk1.reshape(1, -1), lq2.reshape(1, -1), lk2.reshape(1, -1),
      g_subln.reshape(1, d2))


def _mem_attn_kernel(q_ref, mk_ref, mv_ref, g_ref, o_ref, *, n_heads):
    dm = q_ref.shape[1] // n_heads
    scale = dm ** -0.5
    for h in range(n_heads):
        cols = slice(h * dm, (h + 1) * dm)
        kraw = mk_ref[:, cols]
        ms = jnp.mean(kraw * kraw, axis=-1, keepdims=True)
        kn = ((kraw * lax.rsqrt(ms + NORM_EPS)) * g_ref[...]).astype(bf16)
        s = lax.dot_general(q_ref[:, cols], kn, (((1,), (1,)), ((), ())),
                            preferred_element_type=f32) * scale
        m = jnp.max(s, axis=-1, keepdims=True)
        p = jnp.exp(s - m)
        l = jnp.sum(p, axis=-1, keepdims=True)
        o = jnp.dot(p.astype(bf16), mv_ref[:, cols].astype(bf16), preferred_element_type=f32)
        o_ref[:, cols] = (o * (1.0 / l)).astype(o_ref.dtype)


def _mem_attention(qm, mkv, g_knorm, *, tq=512):
    s, wm = qm.shape
    mlen = mkv.shape[0]
    tq = min(tq, s)
    dm = wm // MEM_HEADS
    return pl.pallas_call(
        functools.partial(_mem_attn_kernel, n_heads=MEM_HEADS),
        grid=(s // tq,),
        in_specs=[pl.BlockSpec((tq, wm), lambda i: (i, 0)),
                  pl.BlockSpec((mlen, wm), lambda i: (0, 0)),
                  pl.BlockSpec((mlen, wm), lambda i: (0, 1)),
                  pl.BlockSpec((1, dm), lambda i: (0, 0))],
        out_specs=pl.BlockSpec((tq, wm), lambda i: (i, 0)),
        out_shape=jax.ShapeDtypeStruct((s, wm), bf16),
        compiler_params=_params(1),
        name="mem_attention",
    )(qm, mkv, mkv, g_knorm.reshape(1, dm))


def kernel(x, mem, positions, g_attn_norm, w_in, w_out, g_qnorm, g_knorm, g_mem_qnorm, g_mem_knorm, g_mem_norm, w_mem_kv, lambda_q1, lambda_k1, lambda_q2, lambda_k2, g_subln, g_ffn_norm, w_gate, w_up, w_down):
    b, s, d_model = x.shape
    assert b == 1
    depth = w_in.shape[0]
    mem_width = w_mem_kv.shape[1] // 2
    self_width = w_out.shape[1] - mem_width
    n_heads = self_width // HEAD_DIM
    mem_head = mem_width // MEM_HEADS

    xs = x.reshape(s, d_model)
    tables = _rope_tables(positions.reshape(s))
    mem_n = _rmsnorm(mem.reshape(mem.shape[1], d_model), g_mem_norm)
    mkv = _matmul(mem_n, w_mem_kv.astype(bf16), 0, 2 * mem_width, f32)

    for i in range(depth):
        w_in_i = w_in[i].astype(bf16)
        h = _rmsnorm(xs, g_attn_norm[i])
        qk = _proj_heads(h, w_in_i, 0, 2 * self_width, g_qnorm[i], g_knorm[i], self_width, tables,
                         head=HEAD_DIM, rope=True)
        v = _matmul(h, w_in_i, 2 * self_width, self_width, bf16)
        qm = _proj_heads(h, w_in_i, 3 * self_width, mem_width, g_mem_qnorm[i], g_mem_qnorm[i],
                         mem_width, tables, head=mem_head, rope=False)
        if i % 2 == 0:
            self_out = _moba_attention(qk, v, n_heads)
        else:
            j = i // 2
            lam_init = 0.8 - 0.6 * math.exp(-0.3 * i)
            self_out = _diff_attention(qk, v, n_heads // 2, lambda_q1[j], lambda_k1[j],
                                       lambda_q2[j], lambda_k2[j], g_subln[j], lam_init)
        mem_out = _mem_attention(qm, mkv, g_mem_knorm[i])
        xs = _outproj(self_out, mem_out, w_out[i].astype(bf16), xs)
        f = _rmsnorm(xs, g_ffn_norm[i])
        act = _ffn_up(f, w_gate[i].astype(bf16), w_up[i].astype(bf16))
        xs = _ffn_down(act, w_down[i].astype(bf16), xs)
    return xs.reshape(b, s, d_model)
```

```python
import functools
import math

import jax
import jax.numpy as jnp
from jax import lax
from jax.experimental import pallas as pl
from jax.experimental.pallas import tpu as pltpu

f32 = jnp.float32
bf16 = jnp.bfloat16

HEAD_DIM = 128
MEM_HEADS = 4
ROT_DIM = HEAD_DIM // 4
ROPE_THETA = 500000.0
MOBA_BLOCK = 256
MOBA_TOPK = 3
NORM_EPS = 1e-6
SUBLN_EPS = 1e-5
LOG2_E = 1.4426950408889634
MASK_VALUE = -1e30

LANES = 128
ATTN_TILE = 256
VMEM_LIMIT = 56 * 1024 * 1024


def _tile(n, target, align, col0=0):
    t = (min(target, n) // align) * align
    while n % t or col0 % t:
        t -= align
    return t


def _params(n_grid, flags=None):
    return pltpu.CompilerParams(
        dimension_semantics=("arbitrary",) * n_grid, vmem_limit_bytes=VMEM_LIMIT, flags=flags)


INTERLEAVE_FLAGS = None


def _rmsnorm_kernel(x_ref, g_ref, o_ref, *, eps):
    x = x_ref[...]
    ms = jnp.mean(x * x, axis=-1, keepdims=True)
    o_ref[...] = ((x * lax.rsqrt(ms + eps)) * g_ref[...]).astype(o_ref.dtype)


def _rmsnorm(x, g, *, eps=NORM_EPS, tm=256):
    m, d = x.shape
    tm = min(tm, m)
    return pl.pallas_call(
        functools.partial(_rmsnorm_kernel, eps=eps),
        grid=(m // tm,),
        in_specs=[pl.BlockSpec((tm, d), lambda i: (i, 0)),
                  pl.BlockSpec((1, d), lambda i: (0, 0))],
        out_specs=pl.BlockSpec((tm, d), lambda i: (i, 0)),
        out_shape=jax.ShapeDtypeStruct((m, d), bf16),
        compiler_params=_params(1),
        name="rmsnorm",
    )(x, g.reshape(1, d))


def _rope_tables_kernel(pos_ref, inv_ref, c_ref, sa_ref, sb_ref):
    ang = pos_ref[...].astype(f32) * inv_ref[...]
    lane = lax.broadcasted_iota(jnp.int32, ang.shape, 1)
    c = jnp.cos(ang)
    s = jnp.sin(ang)
    half = ROT_DIM // 2
    c_ref[...] = jnp.where(lane < ROT_DIM, c, 1.0)
    sa_ref[...] = jnp.where((lane >= half) & (lane < ROT_DIM), s, 0.0)
    sb_ref[...] = jnp.where(lane < half, -s, 0.0)


def _rope_tables(positions, *, tm=1024):
    s = positions.shape[0]
    tm = min(tm, s)
    inv = ROPE_THETA ** (-jnp.arange(0, ROT_DIM, 2, dtype=f32) / ROT_DIM)
    inv_lane = jnp.concatenate(
        [inv, inv, jnp.zeros((HEAD_DIM - ROT_DIM,), f32)]).reshape(1, HEAD_DIM)
    spec = pl.BlockSpec((tm, HEAD_DIM), lambda i: (i, 0))
    shape = jax.ShapeDtypeStruct((s, HEAD_DIM), f32)
    return pl.pallas_call(
        _rope_tables_kernel,
        grid=(s // tm,),
        in_specs=[pl.BlockSpec((tm, 1), lambda i: (i, 0)),
                  pl.BlockSpec((1, HEAD_DIM), lambda i: (0, 0))],
        out_specs=[spec, spec, spec],
        out_shape=[shape, shape, shape],
        compiler_params=_params(1),
        name="rope_tables",
    )(positions.reshape(s, 1), inv_lane)


def _proj_heads_kernel(a_ref, w_ref, g1_ref, g2_ref, c_ref, sa_ref, sb_ref, o_ref,
                       *, head, n_first, rope):
    acc = jnp.dot(a_ref[...], w_ref[...], preferred_element_type=f32)
    g = jnp.where(pl.program_id(1) < n_first, g1_ref[...], g2_ref[...])
    half = ROT_DIM // 2
    for h in range(acc.shape[1] // head):
        t = acc[:, h * head:(h + 1) * head]
        ms = jnp.mean(t * t, axis=-1, keepdims=True)
        y = (t * lax.rsqrt(ms + NORM_EPS)) * g
        if rope:
            y = (y * c_ref[...] + pltpu.roll(y, half, axis=1) * sa_ref[...]
                 + pltpu.roll(y, head - half, axis=1) * sb_ref[...])
        o_ref[:, h * head:(h + 1) * head] = y.astype(o_ref.dtype)


def _proj_heads(a, w, col0, n, g1, g2, n_first_cols, tables, *, head, rope, tm=1024, tn=1024):
    m, k = a.shape
    tm = min(tm, m)
    tn = _tile(n, tn, head, math.gcd(col0, n_first_cols))
    c, sa, sb = tables
    off = col0 // tn
    tab_spec = pl.BlockSpec((tm, HEAD_DIM), lambda i, j: (i, 0))
    g_spec = pl.BlockSpec((1, head), lambda i, j: (0, 0))
    return pl.pallas_call(
        functools.partial(_proj_heads_kernel, head=head, n_first=n_first_cols // tn, rope=rope),
        grid=(m // tm, n // tn),
        in_specs=[pl.BlockSpec((tm, k), lambda i, j: (i, 0)),
                  pl.BlockSpec((k, tn), lambda i, j: (0, j + off)),
                  g_spec, g_spec, tab_spec, tab_spec, tab_spec],
        out_specs=pl.BlockSpec((tm, tn), lambda i, j: (i, j)),
        out_shape=jax.ShapeDtypeStruct((m, n), bf16),
        compiler_params=_params(2),
        name="proj_heads",
    )(a, w, g1.reshape(1, head), g2.reshape(1, head), c, sa, sb)


def _matmul_kernel(a_ref, w_ref, o_ref):
    o_ref[...] = jnp.dot(a_ref[...], w_ref[...], preferred_element_type=f32).astype(o_ref.dtype)


def _matmul(a, w, col0, n, out_dtype, *, tm=1024, tn=1024):
    m, k = a.shape
    tm = min(tm, m)
    tn = _tile(n, tn, LANES, col0)
    off = col0 // tn
    return pl.pallas_call(
        _matmul_kernel,
        grid=(m // tm, n // tn),
        in_specs=[pl.BlockSpec((tm, k), lambda i, j: (i, 0)),
                  pl.BlockSpec((k, tn), lambda i, j: (0, j + off))],
        out_specs=pl.BlockSpec((tm, tn), lambda i, j: (i, j)),
        out_shape=jax.ShapeDtypeStruct((m, n), out_dtype),
        compiler_params=_params(2),
        name="matmul",
    )(a, w)


def _outproj_kernel(a1_ref, a2_ref, w1_ref, w2_ref, r_ref, o_ref):
    acc = jnp.dot(a1_ref[...], w1_ref[...], preferred_element_type=f32)
    acc = acc + jnp.dot(a2_ref[...], w2_ref[...], preferred_element_type=f32)
    o_ref[...] = r_ref[...] + acc


def _outproj(a1, a2, w, resid, *, tm=1024, tn=512):
    m, k1 = a1.shape
    k2 = a2.shape[1]
    n = w.shape[1]
    tm = min(tm, m)
    tn = _tile(n, tn, LANES)
    assert k1 % k2 == 0
    return pl.pallas_call(
        _outproj_kernel,
        grid=(m // tm, n // tn),
        in_specs=[pl.BlockSpec((tm, k1), lambda i, j: (i, 0)),
                  pl.BlockSpec((tm, k2), lambda i, j: (i, 0)),
                  pl.BlockSpec((k1, tn), lambda i, j: (0, j)),
                  pl.BlockSpec((k2, tn), lambda i, j: (k1 // k2, j)),
                  pl.BlockSpec((tm, tn), lambda i, j: (i, j))],
        out_specs=pl.BlockSpec((tm, tn), lambda i, j: (i, j)),
        out_shape=jax.ShapeDtypeStruct((m, n), f32),
        compiler_params=_params(2),
        name="outproj",
    )(a1, a2, w, w, resid)


def _ffn_up_kernel(a_ref, wg_ref, wu_ref, o_ref):
    a = a_ref[...]
    g = jnp.dot(a, wg_ref[...], preferred_element_type=f32)
    u = jnp.dot(a, wu_ref[...], preferred_element_type=f32)
    o_ref[...] = ((g * jax.nn.sigmoid(g)) * u).astype(o_ref.dtype)


def _ffn_up(a, wg, wu, *, tm=1024, tn=256):
    m, k = a.shape
    n = wg.shape[1]
    tm = min(tm, m)
    tn = _tile(n, tn, LANES)
    w_spec = pl.BlockSpec((k, tn), lambda i, j: (0, j))
    return pl.pallas_call(
        _ffn_up_kernel,
        grid=(m // tm, n // tn),
        in_specs=[pl.BlockSpec((tm, k), lambda i, j: (i, 0)), w_spec, w_spec],
        out_specs=pl.BlockSpec((tm, tn), lambda i, j: (i, j)),
        out_shape=jax.ShapeDtypeStruct((m, n), bf16),
        compiler_params=_params(2),
        name="ffn_up",
    )(a, wg, wu)


def _ffn_down_kernel(a_ref, w_ref, r_ref, o_ref):
    o_ref[...] = r_ref[...] + jnp.dot(a_ref[...], w_ref[...], preferred_element_type=f32)


def _ffn_down(a, w, resid, *, tm=512, tn=512):
    m, k = a.shape
    n = w.shape[1]
    tm = min(tm, m)
    tn = _tile(n, tn, LANES)
    return pl.pallas_call(
        _ffn_down_kernel,
        grid=(m // tm, n // tn),
        in_specs=[pl.BlockSpec((tm, k), lambda i, j: (i, 0)),
                  pl.BlockSpec((k, tn), lambda i, j: (0, j)),
                  pl.BlockSpec((tm, tn), lambda i, j: (i, j))],
        out_specs=pl.BlockSpec((tm, tn), lambda i, j: (i, j)),
        out_shape=jax.ShapeDtypeStruct((m, n), f32),
        compiler_params=_params(2),
        name="ffn_down",
    )(a, w, resid)


def _transpose_bf16(x):
    return x.astype(f32).T.astype(bf16)


def _flash_streams(streams, i, t_scr, acc_scr, *, scale):
    t = ATTN_TILE
    c = scale * LOG2_E
    n_streams = len(streams)

    def stage_scores(n, slot, mask_fn):
        r0 = pl.multiple_of(n * t, t)
        raws = [jnp.dot(load_k(r0), q_t, preferred_element_type=f32)
                for q_t, load_k, _, _, _ in streams]
        maxima = []
        for idx, raw in enumerate(raws):
            s = mask_fn(idx, raw * c)
            t_scr[slot, idx] = s
            maxima.append(jnp.max(s, axis=0, keepdims=True))
        return maxima

    def accumulate(state, maxima, slot, v_block):
        out = []
        for idx, (_, _, load_v_t, _, dv) in enumerate(streams):
            m, l = state[2 * idx:2 * idx + 2]
            m_new = jnp.maximum(m, maxima[idx])
            alpha = jnp.exp2(m - m_new)
            p = jnp.exp2(t_scr[slot, idx] - m_new)
            l = alpha * l + jnp.sum(p, axis=0, keepdims=True)
            pv = jnp.dot(load_v_t(v_block), p.astype(bf16), preferred_element_type=f32)
            acc_scr[idx, :dv, :] = alpha * acc_scr[idx, :dv, :] + pv
            out += [m_new, l]
        return out

    def past_mask(n):
        def fn(idx, s):
            load_bias = streams[idx][3]
            return s if load_bias is None else s + load_bias(n)
        return fn

    key = lax.broadcasted_iota(jnp.int32, (t, t), 0)
    qry = lax.broadcasted_iota(jnp.int32, (t, t), 1)
    causal = key <= qry
    maxima = stage_scores(i, 0, lambda idx, s: jnp.where(causal, s, MASK_VALUE))
    state = []
    for idx, (_, _, _, _, dv) in enumerate(streams):
        state += [jnp.full((1, t), -jnp.inf, f32), jnp.zeros((1, t), f32)]
        acc_scr[idx, :dv, :] = jnp.zeros((dv, t), f32)

    def step(n, slot, state, maxima):
        maxima_next = stage_scores(n, 1 - slot, past_mask(n))
        state = accumulate(state, maxima, slot, jnp.where(n == 0, i, n - 1))
        return state, maxima_next

    def body(pair, carry):
        state, maxima = list(carry[:2 * n_streams]), list(carry[2 * n_streams:])
        state, maxima = step(2 * pair, 0, state, maxima)
        state, maxima = step(2 * pair + 1, 1, state, maxima)
        return tuple(state + maxima)

    carry = lax.fori_loop(0, i // 2, body, tuple(state + maxima))
    state, maxima = list(carry[:2 * n_streams]), list(carry[2 * n_streams:])

    def odd_tail(state, maxima):
        state, maxima = step(i - 1, 0, state, maxima)
        return tuple(accumulate(state, maxima, 1, i - 1))

    def even_tail(state, maxima):
        return tuple(accumulate(state, maxima, 0, jnp.maximum(i - 1, 0)))

    state = lax.cond(i % 2 == 1, odd_tail, even_tail, state, maxima)
    return [(acc_scr[idx, :dv, :], state[2 * idx + 1])
            for idx, (_, _, _, _, dv) in enumerate(streams)]


def _fill_v_t(v_ref, v_t_ref):
    t = ATTN_TILE
    for n in range(v_t_ref.shape[0]):
        v_t_ref[n] = _transpose_bf16(v_ref[n * t:(n + 1) * t, :])


def _moba_kernel(q_ref, k_ref, v_ref, o_ref, v_t_ref, kmean_ref, bias_ref, t_scr, acc_scr, *, n_sel):
    i = pl.program_id(1)
    t = ATTN_TILE
    d = HEAD_DIM
    nb = v_t_ref.shape[0]
    group = q_ref.shape[1] // d

    @pl.when(i == 0)
    def _():
        _fill_v_t(v_ref, v_t_ref)
        for n in range(nb):
            kb = k_ref[n * t:(n + 1) * t, :].astype(f32)
            kmean_ref[n:n + 1, :] = jnp.sum(kb, axis=0, keepdims=True) / float(t)

    q_t = _transpose_bf16(q_ref[...])
    blk = lax.broadcasted_iota(jnp.int32, (nb, t), 0)
    past = blk < i
    streams = []
    for g in range(group):
        rows = slice(g * d, (g + 1) * d)
        gate = jnp.dot(kmean_ref[:, rows].astype(bf16), q_t[rows], preferred_element_type=f32)
        gt = jnp.where(past, gate, -jnp.inf)
        sel = jnp.zeros(gate.shape, jnp.bool_)
        for _ in range(n_sel):
            mx = jnp.max(gt, axis=0, keepdims=True)
            idx = jnp.min(jnp.where(gt == mx, blk, nb), axis=0, keepdims=True)
            pick = blk == idx
            sel = sel | pick
            gt = jnp.where(pick, -jnp.inf, gt)
        bias_ref[g] = jnp.where(sel & past, 0.0, MASK_VALUE)
        streams.append((
            q_t[rows],
            lambda r0, rows=rows: k_ref[pl.ds(r0, t), rows],
            lambda n, rows=rows: v_t_ref[n, rows, :],
            lambda n, g=g: bias_ref[g, pl.ds(n, 1), :],
            d))

    outs = _flash_streams(streams, i, t_scr, acc_scr, scale=d ** -0.5)
    for g, (acc, l) in enumerate(outs):
        o_ref[:, g * d:(g + 1) * d] = (acc * (1.0 / l)).T.astype(o_ref.dtype)


def _moba_attention(qk, v, n_heads, *, group=4):
    s = qk.shape[0]
    t = ATTN_TILE
    assert t == MOBA_BLOCK and s % t == 0 and n_heads % group == 0
    nb = s // t
    n_sel = min(MOBA_TOPK, max(nb - 1, 1))
    w = group * HEAD_DIM
    n_groups = n_heads // group
    return pl.pallas_call(
        functools.partial(_moba_kernel, n_sel=n_sel),
        grid=(n_groups, nb),
        in_specs=[pl.BlockSpec((t, w), lambda h, i: (i, h)),
                  pl.BlockSpec((s, w), lambda h, i: (0, n_groups + h)),
                  pl.BlockSpec((s, w), lambda h, i: (0, h))],
        out_specs=pl.BlockSpec((t, w), lambda h, i: (i, h)),
        out_shape=jax.ShapeDtypeStruct((s, n_heads * HEAD_DIM), bf16),
        scratch_shapes=[pltpu.VMEM((nb, w, t), bf16),
                        pltpu.VMEM((nb, w), f32),
                        pltpu.VMEM((group, nb, t), f32),
                        pltpu.VMEM((2, group, t, t), f32),
                        pltpu.VMEM((group, HEAD_DIM, t), f32)],
        compiler_params=_params(2, INTERLEAVE_FLAGS),
        name="moba_attention",
    )(qk, qk, v)


def _diff_kernel(q_ref, k_ref, v_ref, lq1_ref, lk1_ref, lq2_ref, lk2_ref, g_ref, o_ref, v_t_ref,
                 t_scr, acc_scr, *, lam_init):
    i = pl.program_id(1)
    t = ATTN_TILE
    d = HEAD_DIM
    group = q_ref.shape[1] // (2 * d)

    @pl.when(i == 0)
    def _():
        _fill_v_t(v_ref, v_t_ref)

    lam = (jnp.exp(jnp.sum(lq1_ref[...] * lk1_ref[...], axis=-1, keepdims=True))
           - jnp.exp(jnp.sum(lq2_ref[...] * lk2_ref[...], axis=-1, keepdims=True))
           + lam_init)
    q_t = _transpose_bf16(q_ref[...])
    streams = []
    for a in range(group):
        vrows = slice(a * 2 * d, (a + 1) * 2 * d)
        for sub in range(2):
            rows = slice((2 * a + sub) * d, (2 * a + sub + 1) * d)
            streams.append((
                q_t[rows],
                lambda r0, rows=rows: k_ref[pl.ds(r0, t), rows],
                lambda n, vrows=vrows: v_t_ref[n, vrows, :],
                None,
                2 * d))
    outs = _flash_streams(streams, i, t_scr, acc_scr, scale=d ** -0.5)
    for a in range(group):
        (acc1, l1), (acc2, l2) = outs[2 * a], outs[2 * a + 1]
        o = acc1 * (1.0 / l1) - lam * (acc2 * (1.0 / l2))
        ms = jnp.mean(o * o, axis=0, keepdims=True)
        y = (o * lax.rsqrt(ms + SUBLN_EPS)).T * g_ref[...]
        o_ref[:, a * 2 * d:(a + 1) * 2 * d] = (y * (1.0 - lam_init)).astype(o_ref.dtype)


def _diff_attention(qk, v, n_heads, lq1, lk1, lq2, lk2, g_subln, lam_init, *, group=2):
    s = qk.shape[0]
    t = ATTN_TILE
    assert s % t == 0 and n_heads % group == 0
    nb = s // t
    d2 = 2 * HEAD_DIM
    w = group * d2
    n_groups = n_heads // group
    vec = pl.BlockSpec((1, HEAD_DIM), lambda h, i: (0, 0))
    return pl.pallas_call(
        functools.partial(_diff_kernel, lam_init=lam_init),
        grid=(n_groups, nb),
        in_specs=[pl.BlockSpec((t, w), lambda h, i: (i, h)),
                  pl.BlockSpec((s, w), lambda h, i: (0, n_groups + h)),
                  pl.BlockSpec((s, w), lambda h, i: (0, h)),
                  vec, vec, vec, vec,
                  pl.BlockSpec((1, d2), lambda h, i: (0, 0))],
        out_specs=pl.BlockSpec((t, w), lambda h, i: (i, h)),
        out_shape=jax.ShapeDtypeStruct((s, n_heads * d2), bf16),
        scratch_shapes=[pltpu.VMEM((nb, w, t), bf16),
                        pltpu.VMEM((2, 2 * group, t, t), f32),
                        pltpu.VMEM((2 * group, d2, t), f32)],
        compiler_params=_params(2, INTERLEAVE_FLAGS),
        name="diff_attention",
    )(qk, qk, v, lq1.reshape(1, -1), lk1.reshape(1, -1), lq2.reshape(1, -1), lk2.reshape(1, -1),
      g_subln.reshape(1, d2))


def _mem_attn_kernel(q_ref, mk_ref, mv_ref, g_ref, o_ref, *, n_heads):
    dm = q_ref.shape[1] // n_heads
    scale = dm ** -0.5
    for h in range(n_heads):
        cols = slice(h * dm, (h + 1) * dm)
        kraw = mk_ref[:, cols]
        ms = jnp.mean(kraw * kraw, axis=-1, keepdims=True)
        kn = ((kraw * lax.rsqrt(ms + NORM_EPS)) * g_ref[...]).astype(bf16)
        s = lax.dot_general(q_ref[:, cols], kn, (((1,), (1,)), ((), ())),
                            preferred_element_type=f32) * scale
        m = jnp.max(s, axis=-1, keepdims=True)
        p = jnp.exp(s - m)
        l = jnp.sum(p, axis=-1, keepdims=True)
        o = jnp.dot(p.astype(bf16), mv_ref[:, cols].astype(bf16), preferred_element_type=f32)
        o_ref[:, cols] = (o * (1.0 / l)).astype(o_ref.dtype)


def _mem_attention(qm, mkv, g_knorm, *, tq=512):
    s, wm = qm.shape
    mlen = mkv.shape[0]
    tq = min(tq, s)
    dm = wm // MEM_HEADS
    return pl.pallas_call(
        functools.partial(_mem_attn_kernel, n_heads=MEM_HEADS),
        grid=(s // tq,),
        in_specs=[pl.BlockSpec((tq, wm), lambda i: (i, 0)),
                  pl.BlockSpec((mlen, wm), lambda i: (0, 0)),
                  pl.BlockSpec((mlen, wm), lambda i: (0, 1)),
                  pl.BlockSpec((1, dm), lambda i: (0, 0))],
        out_specs=pl.BlockSpec((tq, wm), lambda i: (i, 0)),
        out_shape=jax.ShapeDtypeStruct((s, wm), bf16),
        compiler_params=_params(1),
        name="mem_attention",
    )(qm, mkv, mkv, g_knorm.reshape(1, dm))


def kernel(x, mem, positions, g_attn_norm, w_in, w_out, g_qnorm, g_knorm, g_mem_qnorm, g_mem_knorm, g_mem_norm, w_mem_kv, lambda_q1, lambda_k1, lambda_q2, lambda_k2, g_subln, g_ffn_norm, w_gate, w_up, w_down):
    b, s, d_model = x.shape
    assert b == 1
    depth = w_in.shape[0]
    mem_width = w_mem_kv.shape[1] // 2
    self_width = w_out.shape[1] - mem_width
    n_heads = self_width // HEAD_DIM
    mem_head = mem_width // MEM_HEADS

    xs = x.reshape(s, d_model)
    tables = _rope_tables(positions.reshape(s))
    mem_n = _rmsnorm(mem.reshape(mem.shape[1], d_model), g_mem_norm)
    mkv = _matmul(mem_n, w_mem_kv.astype(bf16), 0, 2 * mem_width, f32)

    for i in range(depth):
        w_in_i = w_in[i].astype(bf16)
        h = _rmsnorm(xs, g_attn_norm[i])
        qk = _proj_heads(h, w_in_i, 0, 2 * self_width, g_qnorm[i], g_knorm[i], self_width, tables,
                         head=HEAD_DIM, rope=True)
        v = _matmul(h, w_in_i, 2 * self_width, self_width, bf16)
        qm = _proj_heads(h, w_in_i, 3 * self_width, mem_width, g_mem_qnorm[i], g_mem_qnorm[i],
                         mem_width, tables, head=mem_head, rope=False)
        if i % 2 == 0:
            self_out = _moba_attention(qk, v, n_heads)
        else:
            j = i // 2
            lam_init = 0.8 - 0.6 * math.exp(-0.3 * i)
            self_out = _diff_attention(qk, v, n_heads // 2, lambda_q1[j], lambda_k1[j],
                                       lambda_q2[j], lambda_k2[j], g_subln[j], lam_init)
        mem_out = _mem_attention(qm, mkv, g_mem_knorm[i])
        xs = _outproj(self_out, mem_out, w_out[i].astype(bf16), xs)
        f = _rmsnorm(xs, g_ffn_norm[i])
        act = _ffn_up(f, w_gate[i].astype(bf16), w_up[i].astype(bf16))
        xs = _ffn_down(act, w_down[i].astype(bf16), xs)
    return xs.reshape(b, s, d_model)
```

```python
import functools
import math

import jax
import jax.numpy as jnp
from jax import lax
from jax.experimental import pallas as pl
from jax.experimental.pallas import tpu as pltpu

f32 = jnp.float32
bf16 = jnp.bfloat16

HEAD_DIM = 128
MEM_HEADS = 4
ROT_DIM = HEAD_DIM // 4
ROPE_THETA = 500000.0
MOBA_BLOCK = 256
MOBA_TOPK = 3
NORM_EPS = 1e-6
SUBLN_EPS = 1e-5
LOG2_E = 1.4426950408889634
MASK_VALUE = -1e30

LANES = 128
MXU_COLS = 256
ATTN_TILE = 256
VMEM_LIMIT = 56 * 1024 * 1024


def _tile(n, target, align, col0=0):
    t = (min(target, n) // align) * align
    while n % t or col0 % t:
        t -= align
    return t


def _params(n_grid, flags=None):
    return pltpu.CompilerParams(
        dimension_semantics=("arbitrary",) * n_grid, vmem_limit_bytes=VMEM_LIMIT, flags=flags)


INTERLEAVE_FLAGS = None


def _rmsnorm_kernel(x_ref, g_ref, o_ref, *, eps):
    x = x_ref[...]
    ms = jnp.mean(x * x, axis=-1, keepdims=True)
    o_ref[...] = ((x * lax.rsqrt(ms + eps)) * g_ref[...]).astype(o_ref.dtype)


def _rmsnorm(x, g, *, eps=NORM_EPS, tm=256):
    m, d = x.shape
    tm = min(tm, m)
    return pl.pallas_call(
        functools.partial(_rmsnorm_kernel, eps=eps),
        grid=(m // tm,),
        in_specs=[pl.BlockSpec((tm, d), lambda i: (i, 0)),
                  pl.BlockSpec((1, d), lambda i: (0, 0))],
        out_specs=pl.BlockSpec((tm, d), lambda i: (i, 0)),
        out_shape=jax.ShapeDtypeStruct((m, d), bf16),
        compiler_params=_params(1),
        name="rmsnorm",
    )(x, g.reshape(1, d))


def _rope_tables_kernel(pos_ref, inv_ref, c_ref, sa_ref, sb_ref):
    ang = pos_ref[...].astype(f32) * inv_ref[...]
    lane = lax.broadcasted_iota(jnp.int32, ang.shape, 1)
    c = jnp.cos(ang)
    s = jnp.sin(ang)
    half = ROT_DIM // 2
    c_ref[...] = jnp.where(lane < ROT_DIM, c, 1.0)
    sa_ref[...] = jnp.where((lane >= half) & (lane < ROT_DIM), s, 0.0)
    sb_ref[...] = jnp.where(lane < half, -s, 0.0)


def _rope_tables(positions, *, tm=1024):
    s = positions.shape[0]
    tm = min(tm, s)
    inv = ROPE_THETA ** (-jnp.arange(0, ROT_DIM, 2, dtype=f32) / ROT_DIM)
    inv_lane = jnp.concatenate(
        [inv, inv, jnp.zeros((HEAD_DIM - ROT_DIM,), f32)]).reshape(1, HEAD_DIM)
    spec = pl.BlockSpec((tm, HEAD_DIM), lambda i: (i, 0))
    shape = jax.ShapeDtypeStruct((s, HEAD_DIM), f32)
    return pl.pallas_call(
        _rope_tables_kernel,
        grid=(s // tm,),
        in_specs=[pl.BlockSpec((tm, 1), lambda i: (i, 0)),
                  pl.BlockSpec((1, HEAD_DIM), lambda i: (0, 0))],
        out_specs=[spec, spec, spec],
        out_shape=[shape, shape, shape],
        compiler_params=_params(1),
        name="rope_tables",
    )(positions.reshape(s, 1), inv_lane)


def _proj_heads_kernel(a_ref, w_ref, g1_ref, g2_ref, c_ref, sa_ref, sb_ref, o_ref,
                       *, head, n_first, rope):
    g = jnp.where(pl.program_id(1) < n_first, g1_ref[...], g2_ref[...])
    half = ROT_DIM // 2
    chunk = max(head, MXU_COLS)
    rows = a_ref.shape[0] // 2
    for c0 in range(0, o_ref.shape[1], chunk):
        w = w_ref[:, c0:c0 + chunk].astype(bf16)
        for r0 in (0, rows):
            acc = jnp.dot(a_ref[r0:r0 + rows, :], w, preferred_element_type=f32)
            for h0 in range(0, chunk, head):
                t = acc[:, h0:h0 + head]
                ms = jnp.mean(t * t, axis=-1, keepdims=True)
                y = (t * lax.rsqrt(ms + NORM_EPS)) * g
                if rope:
                    y = (y * c_ref[r0:r0 + rows, :] + pltpu.roll(y, half, axis=1) * sa_ref[r0:r0 + rows, :]
                         + pltpu.roll(y, head - half, axis=1) * sb_ref[r0:r0 + rows, :])
                o_ref[r0:r0 + rows, c0 + h0:c0 + h0 + head] = y.astype(o_ref.dtype)


def _proj_heads(a, w, col0, n, g1, g2, n_first_cols, tables, *, head, rope, tm=1024, tn=1024):
    m, k = a.shape
    tm = min(tm, m)
    tn = _tile(n, tn, head, math.gcd(col0, n_first_cols))
    c, sa, sb = tables
    off = col0 // tn
    tab_spec = pl.BlockSpec((tm, HEAD_DIM), lambda i, j: (i, 0))
    g_spec = pl.BlockSpec((1, head), lambda i, j: (0, 0))
    return pl.pallas_call(
        functools.partial(_proj_heads_kernel, head=head, n_first=n_first_cols // tn, rope=rope),
        grid=(m // tm, n // tn),
        in_specs=[pl.BlockSpec((tm, k), lambda i, j: (i, 0)),
                  pl.BlockSpec((k, tn), lambda i, j: (0, j + off)),
                  g_spec, g_spec, tab_spec, tab_spec, tab_spec],
        out_specs=pl.BlockSpec((tm, tn), lambda i, j: (i, j)),
        out_shape=jax.ShapeDtypeStruct((m, n), bf16),
        compiler_params=_params(2),
        name="proj_heads",
    )(a, w, g1.reshape(1, head), g2.reshape(1, head), c, sa, sb)


def _matmul_kernel(a_ref, w_ref, o_ref):
    o_ref[...] = jnp.dot(a_ref[...], w_ref[...].astype(bf16),
                         preferred_element_type=f32).astype(o_ref.dtype)


def _matmul(a, w, col0, n, out_dtype, *, tm=1024, tn=1024):
    m, k = a.shape
    tm = min(tm, m)
    tn = _tile(n, tn, LANES, col0)
    off = col0 // tn
    return pl.pallas_call(
        _matmul_kernel,
        grid=(m // tm, n // tn),
        in_specs=[pl.BlockSpec((tm, k), lambda i, j: (i, 0)),
                  pl.BlockSpec((k, tn), lambda i, j: (0, j + off))],
        out_specs=pl.BlockSpec((tm, tn), lambda i, j: (i, j)),
        out_shape=jax.ShapeDtypeStruct((m, n), out_dtype),
        compiler_params=_params(2),
        name="matmul",
    )(a, w)


def _outproj_kernel(a1_ref, a2_ref, w1_ref, w2_ref, r_ref, o_ref):
    acc = jnp.dot(a1_ref[...], w1_ref[...].astype(bf16), preferred_element_type=f32)
    acc = acc + jnp.dot(a2_ref[...], w2_ref[...].astype(bf16), preferred_element_type=f32)
    o_ref[...] = r_ref[...] + acc


def _outproj(a1, a2, w, layer, resid, *, tm=1024, tn=512):
    m, k1 = a1.shape
    k2 = a2.shape[1]
    n = w.shape[2]
    tm = min(tm, m)
    tn = _tile(n, tn, LANES)
    assert k1 % k2 == 0
    return pl.pallas_call(
        _outproj_kernel,
        grid=(m // tm, n // tn),
        in_specs=[pl.BlockSpec((tm, k1), lambda i, j: (i, 0)),
                  pl.BlockSpec((tm, k2), lambda i, j: (i, 0)),
                  pl.BlockSpec((None, k1, tn), lambda i, j: (layer, 0, j)),
                  pl.BlockSpec((None, k2, tn), lambda i, j: (layer, k1 // k2, j)),
                  pl.BlockSpec((tm, tn), lambda i, j: (i, j))],
        out_specs=pl.BlockSpec((tm, tn), lambda i, j: (i, j)),
        out_shape=jax.ShapeDtypeStruct((m, n), f32),
        compiler_params=_params(2),
        name="outproj",
    )(a1, a2, w, w, resid)


def _ffn_up_kernel(a_ref, wg_ref, wu_ref, o_ref):
    a = a_ref[...]
    g = jnp.dot(a, wg_ref[...].astype(bf16), preferred_element_type=f32)
    u = jnp.dot(a, wu_ref[...].astype(bf16), preferred_element_type=f32)
    o_ref[...] = ((g * jax.nn.sigmoid(g)) * u).astype(o_ref.dtype)


def _ffn_up(a, wg, wu, layer, *, tm=1024, tn=256):
    m, k = a.shape
    n = wg.shape[2]
    tm = min(tm, m)
    tn = _tile(n, tn, LANES)
    w_spec = pl.BlockSpec((None, k, tn), lambda i, j: (layer, 0, j))
    return pl.pallas_call(
        _ffn_up_kernel,
        grid=(m // tm, n // tn),
        in_specs=[pl.BlockSpec((tm, k), lambda i, j: (i, 0)), w_spec, w_spec],
        out_specs=pl.BlockSpec((tm, tn), lambda i, j: (i, j)),
        out_shape=jax.ShapeDtypeStruct((m, n), bf16),
        compiler_params=_params(2),
        name="ffn_up",
    )(a, wg, wu)


def _ffn_down_kernel(a_ref, w_ref, r_ref, o_ref):
    o_ref[...] = r_ref[...] + jnp.dot(a_ref[...], w_ref[...].astype(bf16),
                                      preferred_element_type=f32)


def _ffn_down(a, w, layer, resid, *, tm=1024, tn=256):
    m, k = a.shape
    n = w.shape[2]
    tm = min(tm, m)
    tn = _tile(n, tn, LANES)
    return pl.pallas_call(
        _ffn_down_kernel,
        grid=(m // tm, n // tn),
        in_specs=[pl.BlockSpec((tm, k), lambda i, j: (i, 0), pipeline_mode=pl.Buffered(1)),
                  pl.BlockSpec((None, k, tn), lambda i, j: (layer, 0, j)),
                  pl.BlockSpec((tm, tn), lambda i, j: (i, j))],
        out_specs=pl.BlockSpec((tm, tn), lambda i, j: (i, j)),
        out_shape=jax.ShapeDtypeStruct((m, n), f32),
        compiler_params=_params(2),
        name="ffn_down",
    )(a, w, resid)


def _transpose_bf16(x):
    return x.astype(f32).T.astype(bf16)


def _flash_streams(streams, i, t_scr, acc_scr, *, scale):
    t = ATTN_TILE
    c = scale * LOG2_E
    n_streams = len(streams)

    def stage_scores(n, slot, mask_fn):
        r0 = pl.multiple_of(n * t, t)
        raws = [jnp.dot(load_k(r0), q_t, preferred_element_type=f32)
                for q_t, load_k, _, _, _ in streams]
        maxima = []
        for idx, raw in enumerate(raws):
            s = mask_fn(idx, raw * c)
            t_scr[slot, idx] = s
            maxima.append(jnp.max(s, axis=0, keepdims=True))
        return maxima

    def accumulate(state, maxima, slot, v_block):
        out = []
        for idx, (_, _, load_v_t, _, dv) in enumerate(streams):
            m, l = state[2 * idx:2 * idx + 2]
            m_new = jnp.maximum(m, maxima[idx])
            alpha = jnp.exp2(m - m_new)
            p = jnp.exp2(t_scr[slot, idx] - m_new)
            l = alpha * l + jnp.sum(p, axis=0, keepdims=True)
            pv = jnp.dot(load_v_t(v_block), p.astype(bf16), preferred_element_type=f32)
            acc_scr[idx, :dv, :] = alpha * acc_scr[idx, :dv, :] + pv
            out += [m_new, l]
        return out

    def past_mask(n):
        def fn(idx, s):
            load_bias = streams[idx][3]
            return s if load_bias is None else s + load_bias(n)
        return fn

    key = lax.broadcasted_iota(jnp.int32, (t, t), 0)
    qry = lax.broadcasted_iota(jnp.int32, (t, t), 1)
    causal = key <= qry
    maxima = stage_scores(i, 0, lambda idx, s: jnp.where(causal, s, MASK_VALUE))
    state = []
    for idx, (_, _, _, _, dv) in enumerate(streams):
        state += [jnp.full((1, t), -jnp.inf, f32), jnp.zeros((1, t), f32)]
        acc_scr[idx, :dv, :] = jnp.zeros((dv, t), f32)

    def step(n, slot, state, maxima):
        maxima_next = stage_scores(n, 1 - slot, past_mask(n))
        state = accumulate(state, maxima, slot, jnp.where(n == 0, i, n - 1))
        return state, maxima_next

    def body(pair, carry):
        state, maxima = list(carry[:2 * n_streams]), list(carry[2 * n_streams:])
        state, maxima = step(2 * pair, 0, state, maxima)
        state, maxima = step(2 * pair + 1, 1, state, maxima)
        return tuple(state + maxima)

    carry = lax.fori_loop(0, i // 2, body, tuple(state + maxima))
    state, maxima = list(carry[:2 * n_streams]), list(carry[2 * n_streams:])

    def odd_tail(state, maxima):
        state, maxima = step(i - 1, 0, state, maxima)
        return tuple(accumulate(state, maxima, 1, i - 1))

    def even_tail(state, maxima):
        return tuple(accumulate(state, maxima, 0, jnp.maximum(i - 1, 0)))

    state = lax.cond(i % 2 == 1, odd_tail, even_tail, state, maxima)
    return [(acc_scr[idx, :dv, :], state[2 * idx + 1])
            for idx, (_, _, _, _, dv) in enumerate(streams)]


def _fill_v_t(v_ref, v_t_ref):
    t = ATTN_TILE
    for n in range(v_t_ref.shape[0]):
        v_t_ref[n] = _transpose_bf16(v_ref[n * t:(n + 1) * t, :])


def _moba_kernel(q_ref, k_ref, v_ref, o_ref, v_t_ref, kmean_ref, bias_ref, t_scr, acc_scr, *, n_sel):
    i = pl.program_id(1)
    t = ATTN_TILE
    d = HEAD_DIM
    nb = v_t_ref.shape[0]
    group = q_ref.shape[1] // d

    @pl.when(i == 0)
    def _():
        _fill_v_t(v_ref, v_t_ref)
        for n in range(nb):
            kb = k_ref[n * t:(n + 1) * t, :].astype(f32)
            kmean_ref[n:n + 1, :] = jnp.sum(kb, axis=0, keepdims=True) / float(t)

    q_t = _transpose_bf16(q_ref[...])
    blk = lax.broadcasted_iota(jnp.int32, (nb, t), 0)
    past = blk < i
    streams = []
    for g in range(group):
        rows = slice(g * d, (g + 1) * d)
        gate = jnp.dot(kmean_ref[:, rows].astype(bf16), q_t[rows], preferred_element_type=f32)
        gt = jnp.where(past, gate, -jnp.inf)
        sel = jnp.zeros(gate.shape, jnp.bool_)
        for _ in range(n_sel):
            mx = jnp.max(gt, axis=0, keepdims=True)
            idx = jnp.min(jnp.where(gt == mx, blk, nb), axis=0, keepdims=True)
            pick = blk == idx
            sel = sel | pick
            gt = jnp.where(pick, -jnp.inf, gt)
        bias_ref[g] = jnp.where(sel & past, 0.0, MASK_VALUE)
        streams.append((
            q_t[rows],
            lambda r0, rows=rows: k_ref[pl.ds(r0, t), rows],
            lambda n, rows=rows: v_t_ref[n, rows, :],
            lambda n, g=g: bias_ref[g, pl.ds(n, 1), :],
            d))

    outs = _flash_streams(streams, i, t_scr, acc_scr, scale=d ** -0.5)
    for g, (acc, l) in enumerate(outs):
        o_ref[:, g * d:(g + 1) * d] = (acc * (1.0 / l)).T.astype(o_ref.dtype)


def _moba_attention(qk, v, n_heads, *, group=4):
    s = qk.shape[0]
    t = ATTN_TILE
    assert t == MOBA_BLOCK and s % t == 0 and n_heads % group == 0
    nb = s // t
    n_sel = min(MOBA_TOPK, max(nb - 1, 1))
    w = group * HEAD_DIM
    n_groups = n_heads // group
    return pl.pallas_call(
        functools.partial(_moba_kernel, n_sel=n_sel),
        grid=(n_groups, nb),
        in_specs=[pl.BlockSpec((t, w), lambda h, i: (i, h)),
                  pl.BlockSpec((s, w), lambda h, i: (0, n_groups + h)),
                  pl.BlockSpec((s, w), lambda h, i: (0, h))],
        out_specs=pl.BlockSpec((t, w), lambda h, i: (i, h)),
        out_shape=jax.ShapeDtypeStruct((s, n_heads * HEAD_DIM), bf16),
        scratch_shapes=[pltpu.VMEM((nb, w, t), bf16),
                        pltpu.VMEM((nb, w), f32),
                        pltpu.VMEM((group, nb, t), f32),
                        pltpu.VMEM((2, group, t, t), f32),
                        pltpu.VMEM((group, HEAD_DIM, t), f32)],
        compiler_params=_params(2, INTERLEAVE_FLAGS),
        name="moba_attention",
    )(qk, qk, v)


def _diff_kernel(q_ref, k_ref, v_ref, lq1_ref, lk1_ref, lq2_ref, lk2_ref, g_ref, o_ref, v_t_ref,
                 t_scr, acc_scr, *, lam_init):
    i = pl.program_id(1)
    t = ATTN_TILE
    d = HEAD_DIM
    group = q_ref.shape[1] // (2 * d)

    @pl.when(i == 0)
    def _():
        _fill_v_t(v_ref, v_t_ref)

    lam = (jnp.exp(jnp.sum(lq1_ref[...] * lk1_ref[...], axis=-1, keepdims=True))
           - jnp.exp(jnp.sum(lq2_ref[...] * lk2_ref[...], axis=-1, keepdims=True))
           + lam_init)
    q_t = _transpose_bf16(q_ref[...])
    streams = []
    for a in range(group):
        vrows = slice(a * 2 * d, (a + 1) * 2 * d)
        for sub in range(2):
            rows = slice((2 * a + sub) * d, (2 * a + sub + 1) * d)
            streams.append((
                q_t[rows],
                lambda r0, rows=rows: k_ref[pl.ds(r0, t), rows],
                lambda n, vrows=vrows: v_t_ref[n, vrows, :],
                None,
                2 * d))
    outs = _flash_streams(streams, i, t_scr, acc_scr, scale=d ** -0.5)
    for a in range(group):
        (acc1, l1), (acc2, l2) = outs[2 * a], outs[2 * a + 1]
        o = acc1 * (1.0 / l1) - lam * (acc2 * (1.0 / l2))
        ms = jnp.mean(o * o, axis=0, keepdims=True)
        y = (o * lax.rsqrt(ms + SUBLN_EPS)).T * g_ref[...]
        o_ref[:, a * 2 * d:(a + 1) * 2 * d] = (y * (1.0 - lam_init)).astype(o_ref.dtype)


def _diff_attention(qk, v, n_heads, lq1, lk1, lq2, lk2, g_subln, lam_init, *, group=2):
    s = qk.shape[0]
    t = ATTN_TILE
    assert s % t == 0 and n_heads % group == 0
    nb = s // t
    d2 = 2 * HEAD_DIM
    w = group * d2
    n_groups = n_heads // group
    vec = pl.BlockSpec((1, HEAD_DIM), lambda h, i: (0, 0))
    return pl.pallas_call(
        functools.partial(_diff_kernel, lam_init=lam_init),
        grid=(n_groups, nb),
        in_specs=[pl.BlockSpec((t, w), lambda h, i: (i, h)),
                  pl.BlockSpec((s, w), lambda h, i: (0, n_groups + h)),
                  pl.BlockSpec((s, w), lambda h, i: (0, h)),
                  vec, vec, vec, vec,
                  pl.BlockSpec((1, d2), lambda h, i: (0, 0))],
        out_specs=pl.BlockSpec((t, w), lambda h, i: (i, h)),
        out_shape=jax.ShapeDtypeStruct((s, n_heads * d2), bf16),
        scratch_shapes=[pltpu.VMEM((nb, w, t), bf16),
                        pltpu.VMEM((2, 2 * group, t, t), f32),
                        pltpu.VMEM((2 * group, d2, t), f32)],
        compiler_params=_params(2, INTERLEAVE_FLAGS),
        name="diff_attention",
    )(qk, qk, v, lq1.reshape(1, -1), lk1.reshape(1, -1), lq2.reshape(1, -1), lk2.reshape(1, -1),
      g_subln.reshape(1, d2))


def _mem_attn_kernel(q_ref, mk_ref, mv_ref, g_ref, o_ref, *, n_heads):
    dm = q_ref.shape[1] // n_heads
    scale = dm ** -0.5
    for h in range(n_heads):
        cols = slice(h * dm, (h + 1) * dm)
        kraw = mk_ref[:, cols]
        ms = jnp.mean(kraw * kraw, axis=-1, keepdims=True)
        kn = ((kraw * lax.rsqrt(ms + NORM_EPS)) * g_ref[...]).astype(bf16)
        s = lax.dot_general(q_ref[:, cols], kn, (((1,), (1,)), ((), ())),
                            preferred_element_type=f32) * scale
        m = jnp.max(s, axis=-1, keepdims=True)
        p = jnp.exp(s - m)
        l = jnp.sum(p, axis=-1, keepdims=True)
        o = jnp.dot(p.astype(bf16), mv_ref[:, cols].astype(bf16), preferred_element_type=f32)
        o_ref[:, cols] = (o * (1.0 / l)).astype(o_ref.dtype)


def _mem_attention(qm, mkv, g_knorm, *, tq=512):
    s, wm = qm.shape
    mlen = mkv.shape[0]
    tq = min(tq, s)
    dm = wm // MEM_HEADS
    return pl.pallas_call(
        functools.partial(_mem_attn_kernel, n_heads=MEM_HEADS),
        grid=(s // tq,),
        in_specs=[pl.BlockSpec((tq, wm), lambda i: (i, 0)),
                  pl.BlockSpec((mlen, wm), lambda i: (0, 0)),
                  pl.BlockSpec((mlen, wm), lambda i: (0, 1)),
                  pl.BlockSpec((1, dm), lambda i: (0, 0))],
        out_specs=pl.BlockSpec((tq, wm), lambda i: (i, 0)),
        out_shape=jax.ShapeDtypeStruct((s, wm), bf16),
        compiler_params=_params(1),
        name="mem_attention",
    )(qm, mkv, mkv, g_knorm.reshape(1, dm))


def kernel(x, mem, positions, g_attn_norm, w_in, w_out, g_qnorm, g_knorm, g_mem_qnorm, g_mem_knorm, g_mem_norm, w_mem_kv, lambda_q1, lambda_k1, lambda_q2, lambda_k2, g_subln, g_ffn_norm, w_gate, w_up, w_down):
    b, s, d_model = x.shape
    assert b == 1
    depth = w_in.shape[0]
    mem_width = w_mem_kv.shape[1] // 2
    self_width = w_out.shape[1] - mem_width
    n_heads = self_width // HEAD_DIM
    mem_head = mem_width // MEM_HEADS

    xs = x.reshape(s, d_model)
    tables = _rope_tables(positions.reshape(s))
    mem_n = _rmsnorm(mem.reshape(mem.shape[1], d_model), g_mem_norm)
    mkv = _matmul(mem_n, w_mem_kv, 0, 2 * mem_width, f32)

    for i in range(depth):
        w_in_i = w_in[i].astype(bf16)
        h = _rmsnorm(xs, g_attn_norm[i])
        qk = _proj_heads(h, w_in_i, 0, 2 * self_width, g_qnorm[i], g_knorm[i], self_width, tables,
                         head=HEAD_DIM, rope=True)
        v = _matmul(h, w_in_i, 2 * self_width, self_width, bf16)
        qm = _proj_heads(h, w_in_i, 3 * self_width, mem_width, g_mem_qnorm[i], g_mem_qnorm[i],
                         mem_width, tables, head=mem_head, rope=False)
        if i % 2 == 0:
            self_out = _moba_attention(qk, v, n_heads)
        else:
            j = i // 2
            lam_init = 0.8 - 0.6 * math.exp(-0.3 * i)
            self_out = _diff_attention(qk, v, n_heads // 2, lambda_q1[j], lambda_k1[j],
                                       lambda_q2[j], lambda_k2[j], g_subln[j], lam_init)
        mem_out = _mem_attention(qm, mkv, g_mem_knorm[i])
        xs = _outproj(self_out, mem_out, w_out, i, xs)
        f = _rmsnorm(xs, g_ffn_norm[i])
        act = _ffn_up(f, w_gate, w_up, i)
        xs = _ffn_down(act, w_down, i, xs)
    return xs.reshape(b, s, d_model)
```

```python
import functools
import math

import jax
import jax.numpy as jnp
from jax import lax
from jax.experimental import pallas as pl
from jax.experimental.pallas import tpu as pltpu

f32 = jnp.float32
bf16 = jnp.bfloat16

HEAD_DIM = 128
MEM_HEADS = 4
ROT_DIM = HEAD_DIM // 4
ROPE_THETA = 500000.0
MOBA_BLOCK = 256
MOBA_TOPK = 3
NORM_EPS = 1e-6
SUBLN_EPS = 1e-5
LOG2_E = 1.4426950408889634
MASK_VALUE = -1e30

LANES = 128
MXU_COLS = 256
ONES_ROWS = 16
ATTN_TILE = 256
VMEM_LIMIT = 56 * 1024 * 1024


def _tile(n, target, align, col0=0):
    t = (min(target, n) // align) * align
    while n % t or col0 % t:
        t -= align
    return t


def _params(n_grid, flags=None):
    return pltpu.CompilerParams(
        dimension_semantics=("arbitrary",) * n_grid, vmem_limit_bytes=VMEM_LIMIT, flags=flags)


INTERLEAVE_FLAGS = None


def _rmsnorm_kernel(x_ref, g_ref, o_ref, *, eps):
    x = x_ref[...]
    ms = jnp.mean(x * x, axis=-1, keepdims=True)
    o_ref[...] = ((x * lax.rsqrt(ms + eps)) * g_ref[...]).astype(o_ref.dtype)


def _rmsnorm(x, g, *, eps=NORM_EPS, tm=256):
    m, d = x.shape
    tm = min(tm, m)
    return pl.pallas_call(
        functools.partial(_rmsnorm_kernel, eps=eps),
        grid=(m // tm,),
        in_specs=[pl.BlockSpec((tm, d), lambda i: (i, 0)),
                  pl.BlockSpec((1, d), lambda i: (0, 0))],
        out_specs=pl.BlockSpec((tm, d), lambda i: (i, 0)),
        out_shape=jax.ShapeDtypeStruct((m, d), bf16),
        compiler_params=_params(1),
        name="rmsnorm",
    )(x, g.reshape(1, d))


def _rope_tables_kernel(pos_ref, inv_ref, c_ref, sa_ref, sb_ref):
    ang = pos_ref[...].astype(f32) * inv_ref[...]
    lane = lax.broadcasted_iota(jnp.int32, ang.shape, 1)
    c = jnp.cos(ang)
    s = jnp.sin(ang)
    half = ROT_DIM // 2
    c_ref[...] = jnp.where(lane < ROT_DIM, c, 1.0)
    sa_ref[...] = jnp.where((lane >= half) & (lane < ROT_DIM), s, 0.0)
    sb_ref[...] = jnp.where(lane < half, -s, 0.0)


def _rope_tables(positions, *, tm=1024):
    s = positions.shape[0]
    tm = min(tm, s)
    inv = ROPE_THETA ** (-jnp.arange(0, ROT_DIM, 2, dtype=f32) / ROT_DIM)
    inv_lane = jnp.concatenate(
        [inv, inv, jnp.zeros((HEAD_DIM - ROT_DIM,), f32)]).reshape(1, HEAD_DIM)
    spec = pl.BlockSpec((tm, HEAD_DIM), lambda i: (i, 0))
    shape = jax.ShapeDtypeStruct((s, HEAD_DIM), f32)
    return pl.pallas_call(
        _rope_tables_kernel,
        grid=(s // tm,),
        in_specs=[pl.BlockSpec((tm, 1), lambda i: (i, 0)),
                  pl.BlockSpec((1, HEAD_DIM), lambda i: (0, 0))],
        out_specs=[spec, spec, spec],
        out_shape=[shape, shape, shape],
        compiler_params=_params(1),
        name="rope_tables",
    )(positions.reshape(s, 1), inv_lane)


def _proj_heads_kernel(a_ref, w_ref, g1_ref, g2_ref, c_ref, sa_ref, sb_ref, o_ref,
                       *, head, n_first, rope):
    g = jnp.where(pl.program_id(1) < n_first, g1_ref[...], g2_ref[...])
    half = ROT_DIM // 2
    chunk = max(head, MXU_COLS)
    rows = a_ref.shape[0] // 2
    for c0 in range(0, o_ref.shape[1], chunk):
        w = w_ref[:, c0:c0 + chunk].astype(bf16)
        for r0 in (0, rows):
            acc = jnp.dot(a_ref[r0:r0 + rows, :], w, preferred_element_type=f32)
            for h0 in range(0, chunk, head):
                t = acc[:, h0:h0 + head]
                ms = jnp.mean(t * t, axis=-1, keepdims=True)
                y = (t * lax.rsqrt(ms + NORM_EPS)) * g
                if rope:
                    y = (y * c_ref[r0:r0 + rows, :] + pltpu.roll(y, half, axis=1) * sa_ref[r0:r0 + rows, :]
                         + pltpu.roll(y, head - half, axis=1) * sb_ref[r0:r0 + rows, :])
                o_ref[r0:r0 + rows, c0 + h0:c0 + h0 + head] = y.astype(o_ref.dtype)


def _proj_heads(a, w, col0, n, g1, g2, n_first_cols, tables, *, head, rope, tm=1024, tn=1024):
    m, k = a.shape
    tm = min(tm, m)
    tn = _tile(n, tn, head, math.gcd(col0, n_first_cols))
    c, sa, sb = tables
    off = col0 // tn
    tab_spec = pl.BlockSpec((tm, HEAD_DIM), lambda i, j: (i, 0))
    g_spec = pl.BlockSpec((1, head), lambda i, j: (0, 0))
    return pl.pallas_call(
        functools.partial(_proj_heads_kernel, head=head, n_first=n_first_cols // tn, rope=rope),
        grid=(m // tm, n // tn),
        in_specs=[pl.BlockSpec((tm, k), lambda i, j: (i, 0)),
                  pl.BlockSpec((k, tn), lambda i, j: (0, j + off)),
                  g_spec, g_spec, tab_spec, tab_spec, tab_spec],
        out_specs=pl.BlockSpec((tm, tn), lambda i, j: (i, j)),
        out_shape=jax.ShapeDtypeStruct((m, n), bf16),
        compiler_params=_params(2),
        name="proj_heads",
    )(a, w, g1.reshape(1, head), g2.reshape(1, head), c, sa, sb)


def _matmul_kernel(a_ref, w_ref, o_ref):
    o_ref[...] = jnp.dot(a_ref[...], w_ref[...].astype(bf16),
                         preferred_element_type=f32).astype(o_ref.dtype)


def _matmul(a, w, col0, n, out_dtype, *, tm=1024, tn=1024):
    m, k = a.shape
    tm = min(tm, m)
    tn = _tile(n, tn, LANES, col0)
    off = col0 // tn
    return pl.pallas_call(
        _matmul_kernel,
        grid=(m // tm, n // tn),
        in_specs=[pl.BlockSpec((tm, k), lambda i, j: (i, 0)),
                  pl.BlockSpec((k, tn), lambda i, j: (0, j + off))],
        out_specs=pl.BlockSpec((tm, tn), lambda i, j: (i, j)),
        out_shape=jax.ShapeDtypeStruct((m, n), out_dtype),
        compiler_params=_params(2),
        name="matmul",
    )(a, w)


def _outproj_kernel(a1_ref, a2_ref, w1_ref, w2_ref, r_ref, o_ref):
    acc = jnp.dot(a1_ref[...], w1_ref[...].astype(bf16), preferred_element_type=f32)
    acc = acc + jnp.dot(a2_ref[...], w2_ref[...].astype(bf16), preferred_element_type=f32)
    o_ref[...] = r_ref[...] + acc


def _outproj(a1, a2, w, layer, resid, *, tm=1024, tn=512):
    m, k1 = a1.shape
    k2 = a2.shape[1]
    n = w.shape[2]
    tm = min(tm, m)
    tn = _tile(n, tn, LANES)
    assert k1 % k2 == 0
    return pl.pallas_call(
        _outproj_kernel,
        grid=(m // tm, n // tn),
        in_specs=[pl.BlockSpec((tm, k1), lambda i, j: (i, 0)),
                  pl.BlockSpec((tm, k2), lambda i, j: (i, 0)),
                  pl.BlockSpec((None, k1, tn), lambda i, j: (layer, 0, j)),
                  pl.BlockSpec((None, k2, tn), lambda i, j: (layer, k1 // k2, j)),
                  pl.BlockSpec((tm, tn), lambda i, j: (i, j))],
        out_specs=pl.BlockSpec((tm, tn), lambda i, j: (i, j)),
        out_shape=jax.ShapeDtypeStruct((m, n), f32),
        compiler_params=_params(2),
        name="outproj",
    )(a1, a2, w, w, resid)


def _ffn_up_kernel(a_ref, wg_ref, wu_ref, o_ref):
    a = a_ref[...]
    g = jnp.dot(a, wg_ref[...].astype(bf16), preferred_element_type=f32)
    u = jnp.dot(a, wu_ref[...].astype(bf16), preferred_element_type=f32)
    o_ref[...] = ((g * jax.nn.sigmoid(g)) * u).astype(o_ref.dtype)


def _ffn_up(a, wg, wu, layer, *, tm=1024, tn=256):
    m, k = a.shape
    n = wg.shape[2]
    tm = min(tm, m)
    tn = _tile(n, tn, LANES)
    w_spec = pl.BlockSpec((None, k, tn), lambda i, j: (layer, 0, j))
    return pl.pallas_call(
        _ffn_up_kernel,
        grid=(m // tm, n // tn),
        in_specs=[pl.BlockSpec((tm, k), lambda i, j: (i, 0)), w_spec, w_spec],
        out_specs=pl.BlockSpec((tm, tn), lambda i, j: (i, j)),
        out_shape=jax.ShapeDtypeStruct((m, n), bf16),
        compiler_params=_params(2),
        name="ffn_up",
    )(a, wg, wu)


def _ffn_down_kernel(a_ref, w_ref, r_ref, o_ref):
    o_ref[...] = r_ref[...] + jnp.dot(a_ref[...], w_ref[...].astype(bf16),
                                      preferred_element_type=f32)


def _ffn_down(a, w, layer, resid, *, tm=1024, tn=256):
    m, k = a.shape
    n = w.shape[2]
    tm = min(tm, m)
    tn = _tile(n, tn, LANES)
    return pl.pallas_call(
        _ffn_down_kernel,
        grid=(m // tm, n // tn),
        in_specs=[pl.BlockSpec((tm, k), lambda i, j: (i, 0), pipeline_mode=pl.Buffered(1)),
                  pl.BlockSpec((None, k, tn), lambda i, j: (layer, 0, j)),
                  pl.BlockSpec((tm, tn), lambda i, j: (i, j))],
        out_specs=pl.BlockSpec((tm, tn), lambda i, j: (i, j)),
        out_shape=jax.ShapeDtypeStruct((m, n), f32),
        compiler_params=_params(2),
        name="ffn_down",
    )(a, w, resid)


def _transpose_bf16(x):
    return x.astype(f32).T.astype(bf16)


def _flash_streams(streams, i, t_scr, acc_scr, *, scale):
    t = ATTN_TILE
    c = scale * LOG2_E
    n_streams = len(streams)
    key = lax.broadcasted_iota(jnp.int32, (t, t), 0)
    qry = lax.broadcasted_iota(jnp.int32, (t, t), 1)
    causal = key <= qry

    def stage_scores(n, slot, diagonal):
        raws = [jnp.dot(load_k(n, diagonal), q_rhs, preferred_element_type=f32)
                for q_rhs, load_k, _, _ in streams]
        maxima = []
        for idx, raw in enumerate(raws):
            s = raw * c
            if diagonal:
                s = jnp.where(causal, s, MASK_VALUE)
            t_scr[slot, idx] = s
            maxima.append(jnp.max(s, axis=0, keepdims=True))
        return maxima

    def accumulate(m_old, maxima, slot, v_block):
        m_out = []
        for idx, (_, _, load_v_t, _) in enumerate(streams):
            m_new = jnp.maximum(m_old[idx], maxima[idx])
            alpha = jnp.exp2(m_old[idx] - m_new)
            p = jnp.exp2(t_scr[slot, idx] - m_new)
            pv = jnp.dot(load_v_t(v_block), p.astype(bf16), preferred_element_type=f32)
            acc_scr[idx] = alpha * acc_scr[idx] + pv
            m_out.append(m_new)
        return m_out

    maxima = stage_scores(i, 0, True)
    m_run = [jnp.full((1, t), -jnp.inf, f32) for _ in streams]
    for idx in range(n_streams):
        acc_scr[idx] = jnp.zeros(acc_scr.shape[1:], f32)

    def step(n, slot, m_run, maxima):
        maxima_next = stage_scores(n, 1 - slot, False)
        m_run = accumulate(m_run, maxima, slot, jnp.where(n == 0, i, n - 1))
        return m_run, maxima_next

    def body(pair, carry):
        m_run, maxima = list(carry[:n_streams]), list(carry[n_streams:])
        m_run, maxima = step(2 * pair, 0, m_run, maxima)
        m_run, maxima = step(2 * pair + 1, 1, m_run, maxima)
        return tuple(m_run + maxima)

    carry = lax.fori_loop(0, i // 2, body, tuple(m_run + maxima))
    m_run, maxima = list(carry[:n_streams]), list(carry[n_streams:])

    @pl.when(i % 2 == 1)
    def _():
        m_mid, maxima_last = step(i - 1, 0, m_run, maxima)
        accumulate(m_mid, maxima_last, 1, i - 1)

    @pl.when(i % 2 == 0)
    def _():
        accumulate(m_run, maxima, 0, jnp.maximum(i - 1, 0))

    return [(acc_scr[idx, :dv, :], acc_scr[idx, dv:dv + 1, :])
            for idx, (_, _, _, dv) in enumerate(streams)]


def _fill_v_t(v_ref, v_t_ref, dv):
    t = ATTN_TILE
    rows = dv + ONES_ROWS
    for n in range(v_t_ref.shape[0]):
        v_t = _transpose_bf16(v_ref[n * t:(n + 1) * t, :])
        for h in range(v_ref.shape[1] // dv):
            v_t_ref[n, h * rows:h * rows + dv, :] = v_t[h * dv:(h + 1) * dv]
            v_t_ref[n, h * rows + dv:(h + 1) * rows, :] = jnp.ones((ONES_ROWS, t), bf16)


def _moba_kernel(q_ref, k_ref, v_ref, o_ref, v_t_ref, kmean_ref, t_scr, acc_scr, *, n_sel):
    i = pl.program_id(1)
    t = ATTN_TILE
    d = HEAD_DIM
    nb = v_t_ref.shape[0]
    group = q_ref.shape[1] // d
    v_rows = d + ONES_ROWS

    @pl.when(i == 0)
    def _():
        _fill_v_t(v_ref, v_t_ref, d)
        for n in range(nb):
            kb = k_ref[n * t:(n + 1) * t, :].astype(f32)
            kmean_ref[n:n + 1, :] = jnp.sum(kb, axis=0, keepdims=True) / float(t)

    q_t = _transpose_bf16(q_ref[...])
    blk = lax.broadcasted_iota(jnp.int32, (nb, t), 0)
    past = blk < i
    lane = lax.broadcasted_iota(jnp.int32, (t, d), 1)

    def block_onehot(n, diagonal):
        if diagonal:
            return jnp.zeros((t, d), bf16)
        return jnp.where(lane == n, 1.0, 0.0).astype(bf16)

    streams = []
    for g in range(group):
        rows = slice(g * d, (g + 1) * d)
        gate = jnp.dot(kmean_ref[:, rows].astype(bf16), q_t[rows], preferred_element_type=f32)
        gt = jnp.where(past, gate, -jnp.inf)
        sel = jnp.zeros(gate.shape, jnp.bool_)
        for _ in range(n_sel):
            mx = jnp.max(gt, axis=0, keepdims=True)
            idx = jnp.min(jnp.where(gt == mx, blk, nb), axis=0, keepdims=True)
            pick = blk == idx
            sel = sel | pick
            gt = jnp.where(pick, -jnp.inf, gt)
        mask_rows = jnp.concatenate(
            [jnp.where(sel & past, 0.0, MASK_VALUE), jnp.zeros((d - nb, t), f32)], axis=0).astype(bf16)
        streams.append((
            jnp.concatenate([q_t[rows], mask_rows], axis=0),
            lambda n, diagonal, rows=rows: jnp.concatenate(
                [k_ref[pl.ds(pl.multiple_of(n * t, t), t), rows], block_onehot(n, diagonal)], axis=1),
            lambda n, g=g: v_t_ref[n, g * v_rows:(g + 1) * v_rows, :],
            d))

    outs = _flash_streams(streams, i, t_scr, acc_scr, scale=d ** -0.5)
    for g, (acc, l) in enumerate(outs):
        o_ref[:, g * d:(g + 1) * d] = (acc * (1.0 / l)).T.astype(o_ref.dtype)


def _moba_attention(qk, v, n_heads, *, group=4):
    s = qk.shape[0]
    t = ATTN_TILE
    assert t == MOBA_BLOCK and s % t == 0 and n_heads % group == 0
    nb = s // t
    assert nb <= HEAD_DIM
    v_rows = HEAD_DIM + ONES_ROWS
    n_sel = min(MOBA_TOPK, max(nb - 1, 1))
    w = group * HEAD_DIM
    n_groups = n_heads // group
    return pl.pallas_call(
        functools.partial(_moba_kernel, n_sel=n_sel),
        grid=(n_groups, nb),
        in_specs=[pl.BlockSpec((t, w), lambda h, i: (i, h)),
                  pl.BlockSpec((s, w), lambda h, i: (0, n_groups + h)),
                  pl.BlockSpec((s, w), lambda h, i: (0, h))],
        out_specs=pl.BlockSpec((t, w), lambda h, i: (i, h)),
        out_shape=jax.ShapeDtypeStruct((s, n_heads * HEAD_DIM), bf16),
        scratch_shapes=[pltpu.VMEM((nb, group * v_rows, t), bf16),
                        pltpu.VMEM((nb, w), f32),
                        pltpu.VMEM((2, group, t, t), f32),
                        pltpu.VMEM((group, v_rows, t), f32)],
        compiler_params=_params(2, INTERLEAVE_FLAGS),
        name="moba_attention",
    )(qk, qk, v)


def _diff_kernel(q_ref, k_ref, v_ref, lq1_ref, lk1_ref, lq2_ref, lk2_ref, g_ref, o_ref, v_t_ref,
                 t_scr, acc_scr, *, lam_init):
    i = pl.program_id(1)
    t = ATTN_TILE
    d = HEAD_DIM
    group = q_ref.shape[1] // (2 * d)

    v_rows = 2 * d + ONES_ROWS

    @pl.when(i == 0)
    def _():
        _fill_v_t(v_ref, v_t_ref, 2 * d)

    lam = (jnp.exp(jnp.sum(lq1_ref[...] * lk1_ref[...], axis=-1, keepdims=True))
           - jnp.exp(jnp.sum(lq2_ref[...] * lk2_ref[...], axis=-1, keepdims=True))
           + lam_init)
    q_t = _transpose_bf16(q_ref[...])
    streams = []
    for a in range(group):
        vrows = slice(a * v_rows, (a + 1) * v_rows)
        for sub in range(2):
            rows = slice((2 * a + sub) * d, (2 * a + sub + 1) * d)
            streams.append((
                q_t[rows],
                lambda n, diagonal, rows=rows: k_ref[pl.ds(pl.multiple_of(n * t, t), t), rows],
                lambda n, vrows=vrows: v_t_ref[n, vrows, :],
                2 * d))
    outs = _flash_streams(streams, i, t_scr, acc_scr, scale=d ** -0.5)
    for a in range(group):
        (acc1, l1), (acc2, l2) = outs[2 * a], outs[2 * a + 1]
        o = acc1 * (1.0 / l1) - lam * (acc2 * (1.0 / l2))
        ms = jnp.mean(o * o, axis=0, keepdims=True)
        y = (o * lax.rsqrt(ms + SUBLN_EPS)).T * g_ref[...]
        o_ref[:, a * 2 * d:(a + 1) * 2 * d] = (y * (1.0 - lam_init)).astype(o_ref.dtype)


def _diff_attention(qk, v, n_heads, lq1, lk1, lq2, lk2, g_subln, lam_init, *, group=2):
    s = qk.shape[0]
    t = ATTN_TILE
    assert s % t == 0 and n_heads % group == 0
    nb = s // t
    d2 = 2 * HEAD_DIM
    w = group * d2
    n_groups = n_heads // group
    vec = pl.BlockSpec((1, HEAD_DIM), lambda h, i: (0, 0))
    return pl.pallas_call(
        functools.partial(_diff_kernel, lam_init=lam_init),
        grid=(n_groups, nb),
        in_specs=[pl.BlockSpec((t, w), lambda h, i: (i, h)),
                  pl.BlockSpec((s, w), lambda h, i: (0, n_groups + h)),
                  pl.BlockSpec((s, w), lambda h, i: (0, h)),
                  vec, vec, vec, vec,
                  pl.BlockSpec((1, d2), lambda h, i: (0, 0))],
        out_specs=pl.BlockSpec((t, w), lambda h, i: (i, h)),
        out_shape=jax.ShapeDtypeStruct((s, n_heads * d2), bf16),
        scratch_shapes=[pltpu.VMEM((nb, group * (d2 + ONES_ROWS), t), bf16),
                        pltpu.VMEM((2, 2 * group, t, t), f32),
                        pltpu.VMEM((2 * group, d2 + ONES_ROWS, t), f32)],
        compiler_params=_params(2, INTERLEAVE_FLAGS),
        name="diff_attention",
    )(qk, qk, v, lq1.reshape(1, -1), lk1.reshape(1, -1), lq2.reshape(1, -1), lk2.reshape(1, -1),
      g_subln.reshape(1, d2))


def _mem_attn_kernel(q_ref, mk_ref, mv_ref, g_ref, o_ref, *, n_heads):
    dm = q_ref.shape[1] // n_heads
    scale = dm ** -0.5
    for h in range(n_heads):
        cols = slice(h * dm, (h + 1) * dm)
        kraw = mk_ref[:, cols]
        ms = jnp.mean(kraw * kraw, axis=-1, keepdims=True)
        kn = ((kraw * lax.rsqrt(ms + NORM_EPS)) * g_ref[...]).astype(bf16)
        s = lax.dot_general(q_ref[:, cols], kn, (((1,), (1,)), ((), ())),
                            preferred_element_type=f32) * scale
        m = jnp.max(s, axis=-1, keepdims=True)
        p = jnp.exp(s - m)
        l = jnp.sum(p, axis=-1, keepdims=True)
        o = jnp.dot(p.astype(bf16), mv_ref[:, cols].astype(bf16), preferred_element_type=f32)
        o_ref[:, cols] = (o * (1.0 / l)).astype(o_ref.dtype)


def _mem_attention(qm, mkv, g_knorm, *, tq=512):
    s, wm = qm.shape
    mlen = mkv.shape[0]
    tq = min(tq, s)
    dm = wm // MEM_HEADS
    return pl.pallas_call(
        functools.partial(_mem_attn_kernel, n_heads=MEM_HEADS),
        grid=(s // tq,),
        in_specs=[pl.BlockSpec((tq, wm), lambda i: (i, 0)),
                  pl.BlockSpec((mlen, wm), lambda i: (0, 0)),
                  pl.BlockSpec((mlen, wm), lambda i: (0, 1)),
                  pl.BlockSpec((1, dm), lambda i: (0, 0))],
        out_specs=pl.BlockSpec((tq, wm), lambda i: (i, 0)),
        out_shape=jax.ShapeDtypeStruct((s, wm), bf16),
        compiler_params=_params(1),
        name="mem_attention",
    )(qm, mkv, mkv, g_knorm.reshape(1, dm))


def kernel(x, mem, positions, g_attn_norm, w_in, w_out, g_qnorm, g_knorm, g_mem_qnorm, g_mem_knorm, g_mem_norm, w_mem_kv, lambda_q1, lambda_k1, lambda_q2, lambda_k2, g_subln, g_ffn_norm, w_gate, w_up, w_down):
    b, s, d_model = x.shape
    assert b == 1
    depth = w_in.shape[0]
    mem_width = w_mem_kv.shape[1] // 2
    self_width = w_out.shape[1] - mem_width
    n_heads = self_width // HEAD_DIM
    mem_head = mem_width // MEM_HEADS

    xs = x.reshape(s, d_model)
    tables = _rope_tables(positions.reshape(s))
    mem_n = _rmsnorm(mem.reshape(mem.shape[1], d_model), g_mem_norm)
    mkv = _matmul(mem_n, w_mem_kv, 0, 2 * mem_width, f32)

    for i in range(depth):
        w_in_i = w_in[i].astype(bf16)
        h = _rmsnorm(xs, g_attn_norm[i])
        qk = _proj_heads(h, w_in_i, 0, 2 * self_width, g_qnorm[i], g_knorm[i], self_width, tables,
                         head=HEAD_DIM, rope=True)
        v = _matmul(h, w_in_i, 2 * self_width, self_width, bf16)
        qm = _proj_heads(h, w_in_i, 3 * self_width, mem_width, g_mem_qnorm[i], g_mem_qnorm[i],
                         mem_width, tables, head=mem_head, rope=False)
        if i % 2 == 0:
            self_out = _moba_attention(qk, v, n_heads)
        else:
            j = i // 2
            lam_init = 0.8 - 0.6 * math.exp(-0.3 * i)
            self_out = _diff_attention(qk, v, n_heads // 2, lambda_q1[j], lambda_k1[j],
                                       lambda_q2[j], lambda_k2[j], g_subln[j], lam_init)
        mem_out = _mem_attention(qm, mkv, g_mem_knorm[i])
        xs = _outproj(self_out, mem_out, w_out, i, xs)
        f = _rmsnorm(xs, g_ffn_norm[i])
        act = _ffn_up(f, w_gate, w_up, i)
        xs = _ffn_down(act, w_down, i, xs)
    return xs.reshape(b, s, d_model)
```

```python
import functools
import math

import jax
import jax.numpy as jnp
from jax import lax
from jax.experimental import pallas as pl
from jax.experimental.pallas import tpu as pltpu

f32 = jnp.float32
bf16 = jnp.bfloat16

HEAD_DIM = 128
MEM_HEADS = 4
ROT_DIM = HEAD_DIM // 4
ROPE_THETA = 500000.0
MOBA_BLOCK = 256
MOBA_TOPK = 3
NORM_EPS = 1e-6
SUBLN_EPS = 1e-5
LOG2_E = 1.4426950408889634
MASK_VALUE = -1e30

LANES = 128
MXU_COLS = 256
ONES_ROWS = 16
ATTN_TILE = 256
VMEM_LIMIT = 56 * 1024 * 1024


def _tile(n, target, align, col0=0):
    t = (min(target, n) // align) * align
    while n % t or col0 % t:
        t -= align
    return t


def _params(n_grid, flags=None):
    return pltpu.CompilerParams(
        dimension_semantics=("arbitrary",) * n_grid, vmem_limit_bytes=VMEM_LIMIT, flags=flags)


INTERLEAVE_FLAGS = None


def _rmsnorm_kernel(x_ref, g_ref, o_ref, *, eps):
    x = x_ref[...]
    ms = jnp.mean(x * x, axis=-1, keepdims=True)
    o_ref[...] = ((x * lax.rsqrt(ms + eps)) * g_ref[...]).astype(o_ref.dtype)


def _rmsnorm(x, g, *, eps=NORM_EPS, tm=256):
    m, d = x.shape
    tm = min(tm, m)
    return pl.pallas_call(
        functools.partial(_rmsnorm_kernel, eps=eps),
        grid=(m // tm,),
        in_specs=[pl.BlockSpec((tm, d), lambda i: (i, 0)),
                  pl.BlockSpec((1, d), lambda i: (0, 0))],
        out_specs=pl.BlockSpec((tm, d), lambda i: (i, 0)),
        out_shape=jax.ShapeDtypeStruct((m, d), bf16),
        compiler_params=_params(1),
        name="rmsnorm",
    )(x, g.reshape(1, d))


def _rope_tables_kernel(pos_ref, inv_ref, c_ref, sa_ref, sb_ref):
    ang = pos_ref[...].astype(f32) * inv_ref[...]
    lane = lax.broadcasted_iota(jnp.int32, ang.shape, 1)
    c = jnp.cos(ang)
    s = jnp.sin(ang)
    half = ROT_DIM // 2
    c_ref[...] = jnp.where(lane < ROT_DIM, c, 1.0)
    sa_ref[...] = jnp.where((lane >= half) & (lane < ROT_DIM), s, 0.0)
    sb_ref[...] = jnp.where(lane < half, -s, 0.0)


def _rope_tables(positions, *, tm=1024):
    s = positions.shape[0]
    tm = min(tm, s)
    inv = ROPE_THETA ** (-jnp.arange(0, ROT_DIM, 2, dtype=f32) / ROT_DIM)
    inv_lane = jnp.concatenate(
        [inv, inv, jnp.zeros((HEAD_DIM - ROT_DIM,), f32)]).reshape(1, HEAD_DIM)
    spec = pl.BlockSpec((tm, HEAD_DIM), lambda i: (i, 0))
    shape = jax.ShapeDtypeStruct((s, HEAD_DIM), f32)
    return pl.pallas_call(
        _rope_tables_kernel,
        grid=(s // tm,),
        in_specs=[pl.BlockSpec((tm, 1), lambda i: (i, 0)),
                  pl.BlockSpec((1, HEAD_DIM), lambda i: (0, 0))],
        out_specs=[spec, spec, spec],
        out_shape=[shape, shape, shape],
        compiler_params=_params(1),
        name="rope_tables",
    )(positions.reshape(s, 1), inv_lane)


def _in_proj_kernel(a_ref, w_ref, gq_ref, gk_ref, gm_ref, c_ref, sa_ref, sb_ref, o_ref,
                    *, q_tiles, qk_tiles, v_tiles, mem_head):
    j = pl.program_id(1)
    half = ROT_DIM // 2
    rows = a_ref.shape[0] // 2

    def tile(head, gain, rope):
        chunk = MXU_COLS if head is None else max(head, MXU_COLS)
        for c0 in range(0, o_ref.shape[1], chunk):
            w = w_ref[:, c0:c0 + chunk].astype(bf16)
            for r0 in (0, rows):
                acc = jnp.dot(a_ref[r0:r0 + rows, :], w, preferred_element_type=f32)
                if head is None:
                    o_ref[r0:r0 + rows, c0:c0 + chunk] = acc.astype(o_ref.dtype)
                    continue
                for h0 in range(0, chunk, head):
                    t = acc[:, h0:h0 + head]
                    ms = jnp.mean(t * t, axis=-1, keepdims=True)
                    y = (t * lax.rsqrt(ms + NORM_EPS)) * gain
                    if rope:
                        y = (y * c_ref[r0:r0 + rows, :]
                             + pltpu.roll(y, half, axis=1) * sa_ref[r0:r0 + rows, :]
                             + pltpu.roll(y, head - half, axis=1) * sb_ref[r0:r0 + rows, :])
                    o_ref[r0:r0 + rows, c0 + h0:c0 + h0 + head] = y.astype(o_ref.dtype)

    @pl.when(j < qk_tiles)
    def _():
        tile(HEAD_DIM, jnp.where(j < q_tiles, gq_ref[...], gk_ref[...]), True)

    @pl.when((j >= qk_tiles) & (j < qk_tiles + v_tiles))
    def _():
        tile(None, None, False)

    @pl.when(j >= qk_tiles + v_tiles)
    def _():
        tile(mem_head, gm_ref[...], False)


def _in_proj(a, w, layer, self_width, mem_width, g_q, g_k, g_m, tables, *, tm=1024, tn=1024):
    m, k = a.shape
    n = w.shape[2]
    mem_head = mem_width // MEM_HEADS
    tm = min(tm, m)
    tn = _tile(n, tn, mem_head, math.gcd(self_width, mem_width))
    c, sa, sb = tables
    tab_spec = pl.BlockSpec((tm, HEAD_DIM), lambda i, j: (i, 0))
    g_spec = pl.BlockSpec((1, HEAD_DIM), lambda i, j: (0, 0))
    return pl.pallas_call(
        functools.partial(_in_proj_kernel, q_tiles=self_width // tn, qk_tiles=2 * self_width // tn,
                          v_tiles=self_width // tn, mem_head=mem_head),
        grid=(m // tm, n // tn),
        in_specs=[pl.BlockSpec((tm, k), lambda i, j: (i, 0), pipeline_mode=pl.Buffered(1)),
                  pl.BlockSpec((None, k, tn), lambda i, j: (layer, 0, j)),
                  g_spec, g_spec, pl.BlockSpec((1, mem_head), lambda i, j: (0, 0)),
                  tab_spec, tab_spec, tab_spec],
        out_specs=pl.BlockSpec((tm, tn), lambda i, j: (i, j)),
        out_shape=jax.ShapeDtypeStruct((m, n), bf16),
        compiler_params=_params(2),
        name="in_proj",
    )(a, w, g_q.reshape(1, HEAD_DIM), g_k.reshape(1, HEAD_DIM), g_m.reshape(1, mem_head), c, sa, sb)


def _matmul_kernel(a_ref, w_ref, o_ref):
    o_ref[...] = jnp.dot(a_ref[...], w_ref[...].astype(bf16),
                         preferred_element_type=f32).astype(o_ref.dtype)


def _matmul(a, w, col0, n, out_dtype, *, tm=1024, tn=1024):
    m, k = a.shape
    tm = min(tm, m)
    tn = _tile(n, tn, LANES, col0)
    off = col0 // tn
    return pl.pallas_call(
        _matmul_kernel,
        grid=(m // tm, n // tn),
        in_specs=[pl.BlockSpec((tm, k), lambda i, j: (i, 0)),
                  pl.BlockSpec((k, tn), lambda i, j: (0, j + off))],
        out_specs=pl.BlockSpec((tm, tn), lambda i, j: (i, j)),
        out_shape=jax.ShapeDtypeStruct((m, n), out_dtype),
        compiler_params=_params(2),
        name="matmul",
    )(a, w)


def _outproj_kernel(a1_ref, a2_ref, w1_ref, w2_ref, r_ref, o_ref):
    acc = jnp.dot(a1_ref[...], w1_ref[...].astype(bf16), preferred_element_type=f32)
    acc = acc + jnp.dot(a2_ref[...], w2_ref[...].astype(bf16), preferred_element_type=f32)
    o_ref[...] = r_ref[...] + acc


def _outproj(a1, a2, w, layer, resid, *, tm=1024, tn=512):
    m, k1 = a1.shape
    k2 = a2.shape[1]
    n = w.shape[2]
    tm = min(tm, m)
    tn = _tile(n, tn, LANES)
    assert k1 % k2 == 0
    return pl.pallas_call(
        _outproj_kernel,
        grid=(m // tm, n // tn),
        in_specs=[pl.BlockSpec((tm, k1), lambda i, j: (i, 0)),
                  pl.BlockSpec((tm, k2), lambda i, j: (i, 0)),
                  pl.BlockSpec((None, k1, tn), lambda i, j: (layer, 0, j)),
                  pl.BlockSpec((None, k2, tn), lambda i, j: (layer, k1 // k2, j)),
                  pl.BlockSpec((tm, tn), lambda i, j: (i, j))],
        out_specs=pl.BlockSpec((tm, tn), lambda i, j: (i, j)),
        out_shape=jax.ShapeDtypeStruct((m, n), f32),
        compiler_params=_params(2),
        name="outproj",
    )(a1, a2, w, w, resid)


def _ffn_up_kernel(a_ref, wg_ref, wu_ref, o_ref):
    a = a_ref[...]
    g = jnp.dot(a, wg_ref[...].astype(bf16), preferred_element_type=f32)
    u = jnp.dot(a, wu_ref[...].astype(bf16), preferred_element_type=f32)
    o_ref[...] = ((g * jax.nn.sigmoid(g)) * u).astype(o_ref.dtype)


def _ffn_up(a, wg, wu, layer, *, tm=1024, tn=256):
    m, k = a.shape
    n = wg.shape[2]
    tm = min(tm, m)
    tn = _tile(n, tn, LANES)
    w_spec = pl.BlockSpec((None, k, tn), lambda i, j: (layer, 0, j))
    return pl.pallas_call(
        _ffn_up_kernel,
        grid=(m // tm, n // tn),
        in_specs=[pl.BlockSpec((tm, k), lambda i, j: (i, 0)), w_spec, w_spec],
        out_specs=pl.BlockSpec((tm, tn), lambda i, j: (i, j)),
        out_shape=jax.ShapeDtypeStruct((m, n), bf16),
        compiler_params=_params(2),
        name="ffn_up",
    )(a, wg, wu)


def _ffn_down_kernel(a_ref, w_ref, r_ref, o_ref):
    o_ref[...] = r_ref[...] + jnp.dot(a_ref[...], w_ref[...].astype(bf16),
                                      preferred_element_type=f32)


def _ffn_down(a, w, layer, resid, *, tm=1024, tn=256):
    m, k = a.shape
    n = w.shape[2]
    tm = min(tm, m)
    tn = _tile(n, tn, LANES)
    return pl.pallas_call(
        _ffn_down_kernel,
        grid=(m // tm, n // tn),
        in_specs=[pl.BlockSpec((tm, k), lambda i, j: (i, 0), pipeline_mode=pl.Buffered(1)),
                  pl.BlockSpec((None, k, tn), lambda i, j: (layer, 0, j)),
                  pl.BlockSpec((tm, tn), lambda i, j: (i, j))],
        out_specs=pl.BlockSpec((tm, tn), lambda i, j: (i, j)),
        out_shape=jax.ShapeDtypeStruct((m, n), f32),
        compiler_params=_params(2),
        name="ffn_down",
    )(a, w, resid)


def _transpose_bf16(x):
    return x.astype(f32).T.astype(bf16)


def _flash_streams(streams, i, t_scr, acc_scr, *, scale):
    t = ATTN_TILE
    c = scale * LOG2_E
    n_streams = len(streams)
    key = lax.broadcasted_iota(jnp.int32, (t, t), 0)
    qry = lax.broadcasted_iota(jnp.int32, (t, t), 1)
    causal = key <= qry

    def stage_scores(n, slot, diagonal):
        raws = [jnp.dot(load_k(n, diagonal), q_rhs, preferred_element_type=f32)
                for q_rhs, load_k, _, _ in streams]
        maxima = []
        for idx, raw in enumerate(raws):
            s = raw * c
            if diagonal:
                s = jnp.where(causal, s, MASK_VALUE)
            t_scr[slot, idx] = s
            maxima.append(jnp.max(s, axis=0, keepdims=True))
        return maxima

    def accumulate(m_old, maxima, slot, v_block):
        m_out = []
        for idx, (_, _, load_v_t, _) in enumerate(streams):
            m_new = jnp.maximum(m_old[idx], maxima[idx])
            alpha = jnp.exp2(m_old[idx] - m_new)
            p = jnp.exp2(t_scr[slot, idx] - m_new)
            pv = jnp.dot(load_v_t(v_block), p.astype(bf16), preferred_element_type=f32)
            acc_scr[idx] = alpha * acc_scr[idx] + pv
            m_out.append(m_new)
        return m_out

    maxima = stage_scores(i, 0, True)
    m_run = [jnp.full((1, t), -jnp.inf, f32) for _ in streams]
    for idx in range(n_streams):
        acc_scr[idx] = jnp.zeros(acc_scr.shape[1:], f32)

    def step(n, slot, m_run, maxima):
        maxima_next = stage_scores(n, 1 - slot, False)
        m_run = accumulate(m_run, maxima, slot, jnp.where(n == 0, i, n - 1))
        return m_run, maxima_next

    def body(pair, carry):
        m_run, maxima = list(carry[:n_streams]), list(carry[n_streams:])
        m_run, maxima = step(2 * pair, 0, m_run, maxima)
        m_run, maxima = step(2 * pair + 1, 1, m_run, maxima)
        return tuple(m_run + maxima)

    carry = lax.fori_loop(0, i // 2, body, tuple(m_run + maxima))
    m_run, maxima = list(carry[:n_streams]), list(carry[n_streams:])

    @pl.when(i % 2 == 1)
    def _():
        m_mid, maxima_last = step(i - 1, 0, m_run, maxima)
        accumulate(m_mid, maxima_last, 1, i - 1)

    @pl.when(i % 2 == 0)
    def _():
        accumulate(m_run, maxima, 0, jnp.maximum(i - 1, 0))

    return [(acc_scr[idx, :dv, :], acc_scr[idx, dv:dv + 1, :])
            for idx, (_, _, _, dv) in enumerate(streams)]


def _fill_v_t(v_ref, v_t_ref, dv):
    t = ATTN_TILE
    rows = dv + ONES_ROWS
    for n in range(v_t_ref.shape[0]):
        v_t = _transpose_bf16(v_ref[n * t:(n + 1) * t, :])
        for h in range(v_ref.shape[1] // dv):
            v_t_ref[n, h * rows:h * rows + dv, :] = v_t[h * dv:(h + 1) * dv]
            v_t_ref[n, h * rows + dv:(h + 1) * rows, :] = jnp.ones((ONES_ROWS, t), bf16)


def _moba_kernel(q_ref, k_ref, v_ref, o_ref, v_t_ref, kmean_ref, t_scr, acc_scr, *, n_sel):
    i = pl.program_id(1)
    t = ATTN_TILE
    d = HEAD_DIM
    nb = v_t_ref.shape[0]
    group = q_ref.shape[1] // d
    v_rows = d + ONES_ROWS

    @pl.when(i == 0)
    def _():
        _fill_v_t(v_ref, v_t_ref, d)
        for n in range(nb):
            kb = k_ref[n * t:(n + 1) * t, :].astype(f32)
            kmean_ref[n:n + 1, :] = jnp.sum(kb, axis=0, keepdims=True) / float(t)

    q_t = _transpose_bf16(q_ref[...])
    blk = lax.broadcasted_iota(jnp.int32, (nb, t), 0)
    past = blk < i
    lane = lax.broadcasted_iota(jnp.int32, (t, d), 1)

    def block_onehot(n, diagonal):
        if diagonal:
            return jnp.zeros((t, d), bf16)
        return jnp.where(lane == n, 1.0, 0.0).astype(bf16)

    streams = []
    for g in range(group):
        rows = slice(g * d, (g + 1) * d)
        gate = jnp.dot(kmean_ref[:, rows].astype(bf16), q_t[rows], preferred_element_type=f32)
        gt = jnp.where(past, gate, -jnp.inf)
        sel = jnp.zeros(gate.shape, jnp.bool_)
        for _ in range(n_sel):
            mx = jnp.max(gt, axis=0, keepdims=True)
            idx = jnp.min(jnp.where(gt == mx, blk, nb), axis=0, keepdims=True)
            pick = blk == idx
            sel = sel | pick
            gt = jnp.where(pick, -jnp.inf, gt)
        mask_rows = jnp.concatenate(
            [jnp.where(sel & past, 0.0, MASK_VALUE), jnp.zeros((d - nb, t), f32)], axis=0).astype(bf16)
        streams.append((
            jnp.concatenate([q_t[rows], mask_rows], axis=0),
            lambda n, diagonal, rows=rows: jnp.concatenate(
                [k_ref[pl.ds(pl.multiple_of(n * t, t), t), rows], block_onehot(n, diagonal)], axis=1),
            lambda n, g=g: v_t_ref[n, g * v_rows:(g + 1) * v_rows, :],
            d))

    outs = _flash_streams(streams, i, t_scr, acc_scr, scale=d ** -0.5)
    for g, (acc, l) in enumerate(outs):
        o_ref[:, g * d:(g + 1) * d] = (acc * (1.0 / l)).T.astype(o_ref.dtype)


def _moba_attention(proj, n_heads, *, group=4):
    s = proj.shape[0]
    t = ATTN_TILE
    assert t == MOBA_BLOCK and s % t == 0 and n_heads % group == 0
    nb = s // t
    assert nb <= HEAD_DIM
    v_rows = HEAD_DIM + ONES_ROWS
    n_sel = min(MOBA_TOPK, max(nb - 1, 1))
    w = group * HEAD_DIM
    n_groups = n_heads // group
    return pl.pallas_call(
        functools.partial(_moba_kernel, n_sel=n_sel),
        grid=(n_groups, nb),
        in_specs=[pl.BlockSpec((t, w), lambda h, i: (i, h)),
                  pl.BlockSpec((s, w), lambda h, i: (0, n_groups + h)),
                  pl.BlockSpec((s, w), lambda h, i: (0, 2 * n_groups + h))],
        out_specs=pl.BlockSpec((t, w), lambda h, i: (i, h)),
        out_shape=jax.ShapeDtypeStruct((s, n_heads * HEAD_DIM), bf16),
        scratch_shapes=[pltpu.VMEM((nb, group * v_rows, t), bf16),
                        pltpu.VMEM((nb, w), f32),
                        pltpu.VMEM((2, group, t, t), f32),
                        pltpu.VMEM((group, v_rows, t), f32)],
        compiler_params=_params(2, INTERLEAVE_FLAGS),
        name="moba_attention",
    )(proj, proj, proj)


def _diff_kernel(q_ref, k_ref, v_ref, lq1_ref, lk1_ref, lq2_ref, lk2_ref, g_ref, o_ref, v_t_ref,
                 t_scr, acc_scr, *, lam_init):
    i = pl.program_id(1)
    t = ATTN_TILE
    d = HEAD_DIM
    group = q_ref.shape[1] // (2 * d)

    v_rows = 2 * d + ONES_ROWS

    @pl.when(i == 0)
    def _():
        _fill_v_t(v_ref, v_t_ref, 2 * d)

    lam = (jnp.exp(jnp.sum(lq1_ref[...] * lk1_ref[...], axis=-1, keepdims=True))
           - jnp.exp(jnp.sum(lq2_ref[...] * lk2_ref[...], axis=-1, keepdims=True))
           + lam_init)
    q_t = _transpose_bf16(q_ref[...])
    streams = []
    for a in range(group):
        vrows = slice(a * v_rows, (a + 1) * v_rows)
        for sub in range(2):
            rows = slice((2 * a + sub) * d, (2 * a + sub + 1) * d)
            streams.append((
                q_t[rows],
                lambda n, diagonal, rows=rows: k_ref[pl.ds(pl.multiple_of(n * t, t), t), rows],
                lambda n, vrows=vrows: v_t_ref[n, vrows, :],
                2 * d))
    outs = _flash_streams(streams, i, t_scr, acc_scr, scale=d ** -0.5)
    for a in range(group):
        (acc1, l1), (acc2, l2) = outs[2 * a], outs[2 * a + 1]
        o = acc1 * (1.0 / l1) - lam * (acc2 * (1.0 / l2))
        ms = jnp.mean(o * o, axis=0, keepdims=True)
        y = (o * lax.rsqrt(ms + SUBLN_EPS)).T * g_ref[...]
        o_ref[:, a * 2 * d:(a + 1) * 2 * d] = (y * (1.0 - lam_init)).astype(o_ref.dtype)


def _diff_attention(proj, n_heads, lq1, lk1, lq2, lk2, g_subln, lam_init, *, group=2):
    s = proj.shape[0]
    t = ATTN_TILE
    assert s % t == 0 and n_heads % group == 0
    nb = s // t
    d2 = 2 * HEAD_DIM
    w = group * d2
    n_groups = n_heads // group
    vec = pl.BlockSpec((1, HEAD_DIM), lambda h, i: (0, 0))
    return pl.pallas_call(
        functools.partial(_diff_kernel, lam_init=lam_init),
        grid=(n_groups, nb),
        in_specs=[pl.BlockSpec((t, w), lambda h, i: (i, h)),
                  pl.BlockSpec((s, w), lambda h, i: (0, n_groups + h)),
                  pl.BlockSpec((s, w), lambda h, i: (0, 2 * n_groups + h)),
                  vec, vec, vec, vec,
                  pl.BlockSpec((1, d2), lambda h, i: (0, 0))],
        out_specs=pl.BlockSpec((t, w), lambda h, i: (i, h)),
        out_shape=jax.ShapeDtypeStruct((s, n_heads * d2), bf16),
        scratch_shapes=[pltpu.VMEM((nb, group * (d2 + ONES_ROWS), t), bf16),
                        pltpu.VMEM((2, 2 * group, t, t), f32),
                        pltpu.VMEM((2 * group, d2 + ONES_ROWS, t), f32)],
        compiler_params=_params(2, INTERLEAVE_FLAGS),
        name="diff_attention",
    )(proj, proj, proj, lq1.reshape(1, -1), lk1.reshape(1, -1), lq2.reshape(1, -1), lk2.reshape(1, -1),
      g_subln.reshape(1, d2))


def _mem_attn_kernel(q_ref, mk_ref, mv_ref, g_ref, o_ref, *, n_heads):
    dm = q_ref.shape[1] // n_heads
    scale = dm ** -0.5
    for h in range(n_heads):
        cols = slice(h * dm, (h + 1) * dm)
        kraw = mk_ref[:, cols]
        ms = jnp.mean(kraw * kraw, axis=-1, keepdims=True)
        kn = ((kraw * lax.rsqrt(ms + NORM_EPS)) * g_ref[...]).astype(bf16)
        s = lax.dot_general(q_ref[:, cols], kn, (((1,), (1,)), ((), ())),
                            preferred_element_type=f32) * scale
        m = jnp.max(s, axis=-1, keepdims=True)
        p = jnp.exp(s - m)
        l = jnp.sum(p, axis=-1, keepdims=True)
        o = jnp.dot(p.astype(bf16), mv_ref[:, cols].astype(bf16), preferred_element_type=f32)
        o_ref[:, cols] = (o * (1.0 / l)).astype(o_ref.dtype)


def _mem_attention(proj, mkv, g_knorm, *, tq=512):
    s = proj.shape[0]
    mlen, wm = mkv.shape[0], mkv.shape[1] // 2
    q_block = proj.shape[1] // wm - 1
    tq = min(tq, s)
    dm = wm // MEM_HEADS
    return pl.pallas_call(
        functools.partial(_mem_attn_kernel, n_heads=MEM_HEADS),
        grid=(s // tq,),
        in_specs=[pl.BlockSpec((tq, wm), lambda i: (i, q_block)),
                  pl.BlockSpec((mlen, wm), lambda i: (0, 0)),
                  pl.BlockSpec((mlen, wm), lambda i: (0, 1)),
                  pl.BlockSpec((1, dm), lambda i: (0, 0))],
        out_specs=pl.BlockSpec((tq, wm), lambda i: (i, 0)),
        out_shape=jax.ShapeDtypeStruct((s, wm), bf16),
        compiler_params=_params(1),
        name="mem_attention",
    )(proj, mkv, mkv, g_knorm.reshape(1, dm))


def kernel(x, mem, positions, g_attn_norm, w_in, w_out, g_qnorm, g_knorm, g_mem_qnorm, g_mem_knorm, g_mem_norm, w_mem_kv, lambda_q1, lambda_k1, lambda_q2, lambda_k2, g_subln, g_ffn_norm, w_gate, w_up, w_down):
    b, s, d_model = x.shape
    assert b == 1
    depth = w_in.shape[0]
    mem_width = w_mem_kv.shape[1] // 2
    self_width = w_out.shape[1] - mem_width
    n_heads = self_width // HEAD_DIM
    mem_head = mem_width // MEM_HEADS

    xs = x.reshape(s, d_model)
    tables = _rope_tables(positions.reshape(s))
    mem_n = _rmsnorm(mem.reshape(mem.shape[1], d_model), g_mem_norm)
    mkv = _matmul(mem_n, w_mem_kv, 0, 2 * mem_width, f32)

    for i in range(depth):
        h = _rmsnorm(xs, g_attn_norm[i])
        proj = _in_proj(h, w_in, i, self_width, mem_width, g_qnorm[i], g_knorm[i], g_mem_qnorm[i],
                        tables)
        if i % 2 == 0:
            self_out = _moba_attention(proj, n_heads)
        else:
            j = i // 2
            lam_init = 0.8 - 0.6 * math.exp(-0.3 * i)
            self_out = _diff_attention(proj, n_heads // 2, lambda_q1[j], lambda_k1[j],
                                       lambda_q2[j], lambda_k2[j], g_subln[j], lam_init)
        mem_out = _mem_attention(proj, mkv, g_mem_knorm[i])
        xs = _outproj(self_out, mem_out, w_out, i, xs)
        f = _rmsnorm(xs, g_ffn_norm[i])
        act = _ffn_up(f, w_gate, w_up, i)
        xs = _ffn_down(act, w_down, i, xs)
    return xs.reshape(b, s, d_model)
```

```python
import functools
import math

import jax
import jax.numpy as jnp
from jax import lax
from jax.experimental import pallas as pl
from jax.experimental.pallas import tpu as pltpu

f32 = jnp.float32
bf16 = jnp.bfloat16

HEAD_DIM = 128
MEM_HEADS = 4
ROT_DIM = HEAD_DIM // 4
ROPE_THETA = 500000.0
MOBA_BLOCK = 256
MOBA_TOPK = 3
NORM_EPS = 1e-6
SUBLN_EPS = 1e-5
LOG2_E = 1.4426950408889634
MASK_VALUE = -1e30

LANES = 128
MXU_COLS = 256
ONES_ROWS = 16
ATTN_TILE = 256
KEY_TILE = 2 * ATTN_TILE
VMEM_LIMIT = 56 * 1024 * 1024


def _tile(n, target, align, col0=0):
    t = (min(target, n) // align) * align
    while n % t or col0 % t:
        t -= align
    return t


def _params(n_grid, flags=None):
    return pltpu.CompilerParams(
        dimension_semantics=("arbitrary",) * n_grid, vmem_limit_bytes=VMEM_LIMIT, flags=flags)


INTERLEAVE_FLAGS = None


def _rmsnorm_kernel(x_ref, g_ref, o_ref, *, eps):
    x = x_ref[...]
    ms = jnp.mean(x * x, axis=-1, keepdims=True)
    o_ref[...] = ((x * lax.rsqrt(ms + eps)) * g_ref[...]).astype(o_ref.dtype)


def _rmsnorm(x, g, *, eps=NORM_EPS, tm=256):
    m, d = x.shape
    tm = min(tm, m)
    return pl.pallas_call(
        functools.partial(_rmsnorm_kernel, eps=eps),
        grid=(m // tm,),
        in_specs=[pl.BlockSpec((tm, d), lambda i: (i, 0)),
                  pl.BlockSpec((1, d), lambda i: (0, 0))],
        out_specs=pl.BlockSpec((tm, d), lambda i: (i, 0)),
        out_shape=jax.ShapeDtypeStruct((m, d), bf16),
        compiler_params=_params(1),
        name="rmsnorm",
    )(x, g.reshape(1, d))


def _rope_tables_kernel(pos_ref, inv_ref, c_ref, sa_ref, sb_ref):
    ang = pos_ref[...].astype(f32) * inv_ref[...]
    lane = lax.broadcasted_iota(jnp.int32, ang.shape, 1)
    c = jnp.cos(ang)
    s = jnp.sin(ang)
    half = ROT_DIM // 2
    c_ref[...] = jnp.where(lane < ROT_DIM, c, 1.0)
    sa_ref[...] = jnp.where((lane >= half) & (lane < ROT_DIM), s, 0.0)
    sb_ref[...] = jnp.where(lane < half, -s, 0.0)


def _rope_tables(positions, *, tm=1024):
    s = positions.shape[0]
    tm = min(tm, s)
    inv = ROPE_THETA ** (-jnp.arange(0, ROT_DIM, 2, dtype=f32) / ROT_DIM)
    inv_lane = jnp.concatenate(
        [inv, inv, jnp.zeros((HEAD_DIM - ROT_DIM,), f32)]).reshape(1, HEAD_DIM)
    spec = pl.BlockSpec((tm, HEAD_DIM), lambda i: (i, 0))
    shape = jax.ShapeDtypeStruct((s, HEAD_DIM), f32)
    return pl.pallas_call(
        _rope_tables_kernel,
        grid=(s // tm,),
        in_specs=[pl.BlockSpec((tm, 1), lambda i: (i, 0)),
                  pl.BlockSpec((1, HEAD_DIM), lambda i: (0, 0))],
        out_specs=[spec, spec, spec],
        out_shape=[shape, shape, shape],
        compiler_params=_params(1),
        name="rope_tables",
    )(positions.reshape(s, 1), inv_lane)


def _in_proj_kernel(a_ref, w_ref, gq_ref, gk_ref, gm_ref, c_ref, sa_ref, sb_ref, o_ref,
                    *, q_tiles, qk_tiles, v_tiles, mem_head):
    j = pl.program_id(1)
    half = ROT_DIM // 2
    rows = a_ref.shape[0] // 2

    def tile(head, gain, rope):
        chunk = MXU_COLS if head is None else max(head, MXU_COLS)
        for c0 in range(0, o_ref.shape[1], chunk):
            w = w_ref[:, c0:c0 + chunk].astype(bf16)
            for r0 in (0, rows):
                acc = jnp.dot(a_ref[r0:r0 + rows, :], w, preferred_element_type=f32)
                if head is None:
                    o_ref[r0:r0 + rows, c0:c0 + chunk] = acc.astype(o_ref.dtype)
                    continue
                for h0 in range(0, chunk, head):
                    t = acc[:, h0:h0 + head]
                    ms = jnp.mean(t * t, axis=-1, keepdims=True)
                    y = (t * lax.rsqrt(ms + NORM_EPS)) * gain
                    if rope:
                        y = (y * c_ref[r0:r0 + rows, :]
                             + pltpu.roll(y, half, axis=1) * sa_ref[r0:r0 + rows, :]
                             + pltpu.roll(y, head - half, axis=1) * sb_ref[r0:r0 + rows, :])
                    o_ref[r0:r0 + rows, c0 + h0:c0 + h0 + head] = y.astype(o_ref.dtype)

    @pl.when(j < qk_tiles)
    def _():
        tile(HEAD_DIM, jnp.where(j < q_tiles, gq_ref[...], gk_ref[...]), True)

    @pl.when((j >= qk_tiles) & (j < qk_tiles + v_tiles))
    def _():
        tile(None, None, False)

    @pl.when(j >= qk_tiles + v_tiles)
    def _():
        tile(mem_head, gm_ref[...], False)


def _in_proj(a, w, layer, self_width, mem_width, g_q, g_k, g_m, tables, *, tm=1024, tn=1024):
    m, k = a.shape
    n = w.shape[2]
    mem_head = mem_width // MEM_HEADS
    tm = min(tm, m)
    tn = _tile(n, tn, mem_head, math.gcd(self_width, mem_width))
    c, sa, sb = tables
    tab_spec = pl.BlockSpec((tm, HEAD_DIM), lambda i, j: (i, 0))
    g_spec = pl.BlockSpec((1, HEAD_DIM), lambda i, j: (0, 0))
    return pl.pallas_call(
        functools.partial(_in_proj_kernel, q_tiles=self_width // tn, qk_tiles=2 * self_width // tn,
                          v_tiles=self_width // tn, mem_head=mem_head),
        grid=(m // tm, n // tn),
        in_specs=[pl.BlockSpec((tm, k), lambda i, j: (i, 0), pipeline_mode=pl.Buffered(1)),
                  pl.BlockSpec((None, k, tn), lambda i, j: (layer, 0, j)),
                  g_spec, g_spec, pl.BlockSpec((1, mem_head), lambda i, j: (0, 0)),
                  tab_spec, tab_spec, tab_spec],
        out_specs=pl.BlockSpec((tm, tn), lambda i, j: (i, j)),
        out_shape=jax.ShapeDtypeStruct((m, n), bf16),
        compiler_params=_params(2),
        name="in_proj",
    )(a, w, g_q.reshape(1, HEAD_DIM), g_k.reshape(1, HEAD_DIM), g_m.reshape(1, mem_head), c, sa, sb)


def _matmul_kernel(a_ref, w_ref, o_ref):
    o_ref[...] = jnp.dot(a_ref[...], w_ref[...].astype(bf16),
                         preferred_element_type=f32).astype(o_ref.dtype)


def _matmul(a, w, col0, n, out_dtype, *, tm=1024, tn=1024):
    m, k = a.shape
    tm = min(tm, m)
    tn = _tile(n, tn, LANES, col0)
    off = col0 // tn
    return pl.pallas_call(
        _matmul_kernel,
        grid=(m // tm, n // tn),
        in_specs=[pl.BlockSpec((tm, k), lambda i, j: (i, 0)),
                  pl.BlockSpec((k, tn), lambda i, j: (0, j + off))],
        out_specs=pl.BlockSpec((tm, tn), lambda i, j: (i, j)),
        out_shape=jax.ShapeDtypeStruct((m, n), out_dtype),
        compiler_params=_params(2),
        name="matmul",
    )(a, w)


def _outproj_kernel(a1_ref, a2_ref, w1_ref, w2_ref, r_ref, o_ref):
    acc = jnp.dot(a1_ref[...], w1_ref[...].astype(bf16), preferred_element_type=f32)
    acc = acc + jnp.dot(a2_ref[...], w2_ref[...].astype(bf16), preferred_element_type=f32)
    o_ref[...] = r_ref[...] + acc


def _outproj(a1, a2, w, layer, resid, *, tm=1024, tn=512):
    m, k1 = a1.shape
    k2 = a2.shape[1]
    n = w.shape[2]
    tm = min(tm, m)
    tn = _tile(n, tn, LANES)
    assert k1 % k2 == 0
    return pl.pallas_call(
        _outproj_kernel,
        grid=(m // tm, n // tn),
        in_specs=[pl.BlockSpec((tm, k1), lambda i, j: (i, 0)),
                  pl.BlockSpec((tm, k2), lambda i, j: (i, 0)),
                  pl.BlockSpec((None, k1, tn), lambda i, j: (layer, 0, j)),
                  pl.BlockSpec((None, k2, tn), lambda i, j: (layer, k1 // k2, j)),
                  pl.BlockSpec((tm, tn), lambda i, j: (i, j))],
        out_specs=pl.BlockSpec((tm, tn), lambda i, j: (i, j)),
        out_shape=jax.ShapeDtypeStruct((m, n), f32),
        compiler_params=_params(2),
        name="outproj",
    )(a1, a2, w, w, resid)


def _ffn_up_kernel(a_ref, wg_ref, wu_ref, o_ref):
    a = a_ref[...]
    g = jnp.dot(a, wg_ref[...].astype(bf16), preferred_element_type=f32)
    u = jnp.dot(a, wu_ref[...].astype(bf16), preferred_element_type=f32)
    o_ref[...] = ((g * jax.nn.sigmoid(g)) * u).astype(o_ref.dtype)


def _ffn_up(a, wg, wu, layer, *, tm=2048, tn=256):
    m, k = a.shape
    n = wg.shape[2]
    tm = min(tm, m)
    tn = _tile(n, tn, LANES)
    w_spec = pl.BlockSpec((None, k, tn), lambda i, j: (layer, 0, j))
    return pl.pallas_call(
        _ffn_up_kernel,
        grid=(m // tm, n // tn),
        in_specs=[pl.BlockSpec((tm, k), lambda i, j: (i, 0)), w_spec, w_spec],
        out_specs=pl.BlockSpec((tm, tn), lambda i, j: (i, j)),
        out_shape=jax.ShapeDtypeStruct((m, n), bf16),
        compiler_params=_params(2),
        name="ffn_up",
    )(a, wg, wu)


def _ffn_down_kernel(a_ref, w_ref, r_ref, o_ref):
    o_ref[...] = r_ref[...] + jnp.dot(a_ref[...], w_ref[...].astype(bf16),
                                      preferred_element_type=f32)


def _ffn_down(a, w, layer, resid, *, tm=1024, tn=256):
    m, k = a.shape
    n = w.shape[2]
    tm = min(tm, m)
    tn = _tile(n, tn, LANES)
    return pl.pallas_call(
        _ffn_down_kernel,
        grid=(m // tm, n // tn),
        in_specs=[pl.BlockSpec((tm, k), lambda i, j: (i, 0), pipeline_mode=pl.Buffered(1)),
                  pl.BlockSpec((None, k, tn), lambda i, j: (layer, 0, j)),
                  pl.BlockSpec((tm, tn), lambda i, j: (i, j))],
        out_specs=pl.BlockSpec((tm, tn), lambda i, j: (i, j)),
        out_shape=jax.ShapeDtypeStruct((m, n), f32),
        compiler_params=_params(2),
        name="ffn_down",
    )(a, w, resid)


def _transpose_bf16(x):
    return x.astype(f32).T.astype(bf16)


def _flash_streams(streams, i, t_scr, acc_scr, *, scale):
    t, tk = ATTN_TILE, KEY_TILE
    c = scale * LOG2_E
    n_streams = len(streams)
    n_past = i // 2
    key = lax.broadcasted_iota(jnp.int32, (tk, t), 0)
    qry = lax.broadcasted_iota(jnp.int32, (tk, t), 1)
    causal = key - qry <= (i % 2) * t

    def stage_scores(n, slot, diagonal):
        raws = [jnp.dot(load_k(n, diagonal), q_rhs, preferred_element_type=f32)
                for q_rhs, load_k, _, _ in streams]
        maxima = []
        for idx, raw in enumerate(raws):
            s = raw * c
            if diagonal:
                s = jnp.where(causal, s, MASK_VALUE)
            t_scr[slot, idx] = s
            maxima.append(jnp.max(s, axis=0, keepdims=True))
        return maxima

    def accumulate(m_old, maxima, slot, v_pair):
        m_out = []
        for idx, (_, _, load_v_t, _) in enumerate(streams):
            m_new = jnp.maximum(m_old[idx], maxima[idx])
            alpha = jnp.exp2(m_old[idx] - m_new)
            p = jnp.exp2(t_scr[slot, idx] - m_new)
            pv = jnp.dot(load_v_t(v_pair), p.astype(bf16), preferred_element_type=f32)
            acc_scr[idx] = alpha * acc_scr[idx] + pv
            m_out.append(m_new)
        return m_out

    maxima = stage_scores(n_past, 0, True)
    m_run = [jnp.full((1, t), -jnp.inf, f32) for _ in streams]
    for idx in range(n_streams):
        acc_scr[idx] = jnp.zeros(acc_scr.shape[1:], f32)

    def step(n, slot, m_run, maxima):
        maxima_next = stage_scores(n, 1 - slot, False)
        m_run = accumulate(m_run, maxima, slot, jnp.where(n == 0, n_past, n - 1))
        return m_run, maxima_next

    def body(trip, carry):
        m_run, maxima = list(carry[:n_streams]), list(carry[n_streams:])
        m_run, maxima = step(2 * trip, 0, m_run, maxima)
        m_run, maxima = step(2 * trip + 1, 1, m_run, maxima)
        return tuple(m_run + maxima)

    carry = lax.fori_loop(0, n_past // 2, body, tuple(m_run + maxima))
    m_run, maxima = list(carry[:n_streams]), list(carry[n_streams:])

    @pl.when(n_past % 2 == 1)
    def _():
        m_mid, maxima_last = step(n_past - 1, 0, m_run, maxima)
        accumulate(m_mid, maxima_last, 1, n_past - 1)

    @pl.when(n_past % 2 == 0)
    def _():
        accumulate(m_run, maxima, 0, jnp.maximum(n_past - 1, 0))

    return [(acc_scr[idx, :dv, :], acc_scr[idx, dv:dv + 1, :])
            for idx, (_, _, _, dv) in enumerate(streams)]


def _fill_v_t(v_ref, v_t_ref, dv):
    t = KEY_TILE
    rows = dv + ONES_ROWS
    for n in range(v_t_ref.shape[0]):
        v_t = _transpose_bf16(v_ref[n * t:(n + 1) * t, :])
        for h in range(v_ref.shape[1] // dv):
            v_t_ref[n, h * rows:h * rows + dv, :] = v_t[h * dv:(h + 1) * dv]
            v_t_ref[n, h * rows + dv:(h + 1) * rows, :] = jnp.ones((ONES_ROWS, t), bf16)


def _moba_kernel(q_ref, k_ref, v_ref, o_ref, v_t_ref, kmean_ref, t_scr, acc_scr, *, n_sel):
    i = pl.program_id(1)
    t, tk = ATTN_TILE, KEY_TILE
    d = HEAD_DIM
    nb = kmean_ref.shape[0]
    group = q_ref.shape[1] // d
    v_rows = d + ONES_ROWS

    @pl.when(i == 0)
    def _():
        _fill_v_t(v_ref, v_t_ref, d)
        for n in range(nb):
            kb = k_ref[n * t:(n + 1) * t, :].astype(f32)
            kmean_ref[n:n + 1, :] = jnp.sum(kb, axis=0, keepdims=True) / float(t)

    q_t = _transpose_bf16(q_ref[...])
    blk = lax.broadcasted_iota(jnp.int32, (nb, t), 0)
    past = blk < i
    lane = lax.broadcasted_iota(jnp.int32, (tk, d), 1)
    second = lax.broadcasted_iota(jnp.int32, (tk, d), 0) >= t

    def block_onehot(n, diagonal):
        if diagonal:
            block = jnp.where(second | (i % 2 == 0), -1, 2 * n)
        else:
            block = 2 * n + second.astype(jnp.int32)
        return jnp.where(lane == block, 1.0, 0.0).astype(bf16)

    streams = []
    for g in range(group):
        rows = slice(g * d, (g + 1) * d)
        gate = jnp.dot(kmean_ref[:, rows].astype(bf16), q_t[rows], preferred_element_type=f32)
        gt = jnp.where(past, gate, -jnp.inf)
        sel = jnp.zeros(gate.shape, jnp.bool_)
        for _ in range(n_sel):
            mx = jnp.max(gt, axis=0, keepdims=True)
            idx = jnp.min(jnp.where(gt == mx, blk, nb), axis=0, keepdims=True)
            pick = blk == idx
            sel = sel | pick
            gt = jnp.where(pick, -jnp.inf, gt)
        mask_rows = jnp.concatenate(
            [jnp.where(sel & past, 0.0, MASK_VALUE), jnp.zeros((d - nb, t), f32)], axis=0).astype(bf16)
        streams.append((
            jnp.concatenate([q_t[rows], mask_rows], axis=0),
            lambda n, diagonal, rows=rows: jnp.concatenate(
                [k_ref[pl.ds(pl.multiple_of(n * tk, tk), tk), rows], block_onehot(n, diagonal)], axis=1),
            lambda n, g=g: v_t_ref[n, g * v_rows:(g + 1) * v_rows, :],
            d))

    outs = _flash_streams(streams, i, t_scr, acc_scr, scale=d ** -0.5)
    for g, (acc, l) in enumerate(outs):
        o_ref[:, g * d:(g + 1) * d] = (acc * (1.0 / l)).T.astype(o_ref.dtype)


def _moba_attention(proj, n_heads, *, group=4):
    s = proj.shape[0]
    t = ATTN_TILE
    assert t == MOBA_BLOCK and s % KEY_TILE == 0 and n_heads % group == 0
    nb = s // t
    assert nb <= HEAD_DIM
    v_rows = HEAD_DIM + ONES_ROWS
    n_sel = min(MOBA_TOPK, max(nb - 1, 1))
    w = group * HEAD_DIM
    n_groups = n_heads // group
    return pl.pallas_call(
        functools.partial(_moba_kernel, n_sel=n_sel),
        grid=(n_groups, nb),
        in_specs=[pl.BlockSpec((t, w), lambda h, i: (i, h)),
                  pl.BlockSpec((s, w), lambda h, i: (0, n_groups + h)),
                  pl.BlockSpec((s, w), lambda h, i: (0, 2 * n_groups + h))],
        out_specs=pl.BlockSpec((t, w), lambda h, i: (i, h)),
        out_shape=jax.ShapeDtypeStruct((s, n_heads * HEAD_DIM), bf16),
        scratch_shapes=[pltpu.VMEM((s // KEY_TILE, group * v_rows, KEY_TILE), bf16),
                        pltpu.VMEM((nb, w), f32),
                        pltpu.VMEM((2, group, KEY_TILE, t), f32),
                        pltpu.VMEM((group, v_rows, t), f32)],
        compiler_params=_params(2, INTERLEAVE_FLAGS),
        name="moba_attention",
    )(proj, proj, proj)


def _diff_kernel(q_ref, k_ref, v_ref, lq1_ref, lk1_ref, lq2_ref, lk2_ref, g_ref, o_ref, v_t_ref,
                 t_scr, acc_scr, *, lam_init):
    i = pl.program_id(1)
    tk = KEY_TILE
    d = HEAD_DIM
    group = q_ref.shape[1] // (2 * d)

    v_rows = 2 * d + ONES_ROWS

    @pl.when(i == 0)
    def _():
        _fill_v_t(v_ref, v_t_ref, 2 * d)

    lam = (jnp.exp(jnp.sum(lq1_ref[...] * lk1_ref[...], axis=-1, keepdims=True))
           - jnp.exp(jnp.sum(lq2_ref[...] * lk2_ref[...], axis=-1, keepdims=True))
           + lam_init)
    q_t = _transpose_bf16(q_ref[...])
    streams = []
    for a in range(group):
        vrows = slice(a * v_rows, (a + 1) * v_rows)
        for sub in range(2):
            rows = slice((2 * a + sub) * d, (2 * a + sub + 1) * d)
            streams.append((
                q_t[rows],
                lambda n, diagonal, rows=rows: k_ref[pl.ds(pl.multiple_of(n * tk, tk), tk), rows],
                lambda n, vrows=vrows: v_t_ref[n, vrows, :],
                2 * d))
    outs = _flash_streams(streams, i, t_scr, acc_scr, scale=d ** -0.5)
    for a in range(group):
        (acc1, l1), (acc2, l2) = outs[2 * a], outs[2 * a + 1]
        o = acc1 * (1.0 / l1) - lam * (acc2 * (1.0 / l2))
        ms = jnp.mean(o * o, axis=0, keepdims=True)
        y = (o * lax.rsqrt(ms + SUBLN_EPS)).T * g_ref[...]
        o_ref[:, a * 2 * d:(a + 1) * 2 * d] = (y * (1.0 - lam_init)).astype(o_ref.dtype)


def _diff_attention(proj, n_heads, lq1, lk1, lq2, lk2, g_subln, lam_init, *, group=2):
    s = proj.shape[0]
    t = ATTN_TILE
    assert s % KEY_TILE == 0 and n_heads % group == 0
    nb = s // t
    d2 = 2 * HEAD_DIM
    w = group * d2
    n_groups = n_heads // group
    vec = pl.BlockSpec((1, HEAD_DIM), lambda h, i: (0, 0))
    return pl.pallas_call(
        functools.partial(_diff_kernel, lam_init=lam_init),
        grid=(n_groups, nb),
        in_specs=[pl.BlockSpec((t, w), lambda h, i: (i, h)),
                  pl.BlockSpec((s, w), lambda h, i: (0, n_groups + h)),
                  pl.BlockSpec((s, w), lambda h, i: (0, 2 * n_groups + h)),
                  vec, vec, vec, vec,
                  pl.BlockSpec((1, d2), lambda h, i: (0, 0))],
        out_specs=pl.BlockSpec((t, w), lambda h, i: (i, h)),
        out_shape=jax.ShapeDtypeStruct((s, n_heads * d2), bf16),
        scratch_shapes=[pltpu.VMEM((s // KEY_TILE, group * (d2 + ONES_ROWS), KEY_TILE), bf16),
                        pltpu.VMEM((2, 2 * group, KEY_TILE, t), f32),
                        pltpu.VMEM((2 * group, d2 + ONES_ROWS, t), f32)],
        compiler_params=_params(2, INTERLEAVE_FLAGS),
        name="diff_attention",
    )(proj, proj, proj, lq1.reshape(1, -1), lk1.reshape(1, -1), lq2.reshape(1, -1), lk2.reshape(1, -1),
      g_subln.reshape(1, d2))


def _mem_attn_kernel(q_ref, mk_ref, mv_ref, g_ref, o_ref, *, n_heads):
    dm = q_ref.shape[1] // n_heads
    scale = dm ** -0.5
    for h in range(n_heads):
        cols = slice(h * dm, (h + 1) * dm)
        kraw = mk_ref[:, cols]
        ms = jnp.mean(kraw * kraw, axis=-1, keepdims=True)
        kn = ((kraw * lax.rsqrt(ms + NORM_EPS)) * g_ref[...]).astype(bf16)
        s = lax.dot_general(q_ref[:, cols], kn, (((1,), (1,)), ((), ())),
                            preferred_element_type=f32) * scale
        m = jnp.max(s, axis=-1, keepdims=True)
        p = jnp.exp(s - m)
        l = jnp.sum(p, axis=-1, keepdims=True)
        o = jnp.dot(p.astype(bf16), mv_ref[:, cols].astype(bf16), preferred_element_type=f32)
        o_ref[:, cols] = (o * (1.0 / l)).astype(o_ref.dtype)


def _mem_attention(proj, mkv, g_knorm, *, tq=512):
    s = proj.shape[0]
    mlen, wm = mkv.shape[0], mkv.shape[1] // 2
    q_block = proj.shape[1] // wm - 1
    tq = min(tq, s)
    dm = wm // MEM_HEADS
    return pl.pallas_call(
        functools.partial(_mem_attn_kernel, n_heads=MEM_HEADS),
        grid=(s // tq,),
        in_specs=[pl.BlockSpec((tq, wm), lambda i: (i, q_block)),
                  pl.BlockSpec((mlen, wm), lambda i: (0, 0)),
                  pl.BlockSpec((mlen, wm), lambda i: (0, 1)),
                  pl.BlockSpec((1, dm), lambda i: (0, 0))],
        out_specs=pl.BlockSpec((tq, wm), lambda i: (i, 0)),
        out_shape=jax.ShapeDtypeStruct((s, wm), bf16),
        compiler_params=_params(1),
        name="mem_attention",
    )(proj, mkv, mkv, g_knorm.reshape(1, dm))


def kernel(x, mem, positions, g_attn_norm, w_in, w_out, g_qnorm, g_knorm, g_mem_qnorm, g_mem_knorm, g_mem_norm, w_mem_kv, lambda_q1, lambda_k1, lambda_q2, lambda_k2, g_subln, g_ffn_norm, w_gate, w_up, w_down):
    b, s, d_model = x.shape
    assert b == 1
    depth = w_in.shape[0]
    mem_width = w_mem_kv.shape[1] // 2
    self_width = w_out.shape[1] - mem_width
    n_heads = self_width // HEAD_DIM
    mem_head = mem_width // MEM_HEADS

    xs = x.reshape(s, d_model)
    tables = _rope_tables(positions.reshape(s))
    mem_n = _rmsnorm(mem.reshape(mem.shape[1], d_model), g_mem_norm)
    mkv = _matmul(mem_n, w_mem_kv, 0, 2 * mem_width, f32)

    for i in range(depth):
        h = _rmsnorm(xs, g_attn_norm[i])
        proj = _in_proj(h, w_in, i, self_width, mem_width, g_qnorm[i], g_knorm[i], g_mem_qnorm[i],
                        tables)
        if i % 2 == 0:
            self_out = _moba_attention(proj, n_heads)
        else:
            j = i // 2
            lam_init = 0.8 - 0.6 * math.exp(-0.3 * i)
            self_out = _diff_attention(proj, n_heads // 2, lambda_q1[j], lambda_k1[j],
                                       lambda_q2[j], lambda_k2[j], g_subln[j], lam_init)
        mem_out = _mem_attention(proj, mkv, g_mem_knorm[i])
        xs = _outproj(self_out, mem_out, w_out, i, xs)
        f = _rmsnorm(xs, g_ffn_norm[i])
        act = _ffn_up(f, w_gate, w_up, i)
        xs = _ffn_down(act, w_down, i, xs)
    return xs.reshape(b, s, d_model)
```

```python
import functools
import math

import jax
import jax.numpy as jnp
from jax import lax
from jax.experimental import pallas as pl
from jax.experimental.pallas import tpu as pltpu

f32 = jnp.float32
bf16 = jnp.bfloat16

HEAD_DIM = 128
MEM_HEADS = 4
ROT_DIM = HEAD_DIM // 4
ROPE_THETA = 500000.0
MOBA_BLOCK = 256
MOBA_TOPK = 3
NORM_EPS = 1e-6
SUBLN_EPS = 1e-5
LOG2_E = 1.4426950408889634
MASK_VALUE = -1e30

LANES = 128
MXU_COLS = 256
ONES_ROWS = 16
ATTN_TILE = 256
KEY_TILE = 2 * ATTN_TILE
QUERY_TILE = KEY_TILE
VMEM_LIMIT = 56 * 1024 * 1024


def _tile(n, target, align, col0=0):
    t = (min(target, n) // align) * align
    while n % t or col0 % t:
        t -= align
    return t


def _params(n_grid, flags=None):
    return pltpu.CompilerParams(
        dimension_semantics=("arbitrary",) * n_grid, vmem_limit_bytes=VMEM_LIMIT, flags=flags)


INTERLEAVE_FLAGS = None


def _rmsnorm_kernel(x_ref, g_ref, o_ref, *, eps):
    x = x_ref[...]
    ms = jnp.mean(x * x, axis=-1, keepdims=True)
    o_ref[...] = ((x * lax.rsqrt(ms + eps)) * g_ref[...]).astype(o_ref.dtype)


def _rmsnorm(x, g, *, eps=NORM_EPS, tm=256):
    m, d = x.shape
    tm = min(tm, m)
    return pl.pallas_call(
        functools.partial(_rmsnorm_kernel, eps=eps),
        grid=(m // tm,),
        in_specs=[pl.BlockSpec((tm, d), lambda i: (i, 0)),
                  pl.BlockSpec((1, d), lambda i: (0, 0))],
        out_specs=pl.BlockSpec((tm, d), lambda i: (i, 0)),
        out_shape=jax.ShapeDtypeStruct((m, d), bf16),
        compiler_params=_params(1),
        name="rmsnorm",
    )(x, g.reshape(1, d))


def _rope_tables_kernel(pos_ref, inv_ref, c_ref, sa_ref, sb_ref):
    ang = pos_ref[...].astype(f32) * inv_ref[...]
    lane = lax.broadcasted_iota(jnp.int32, ang.shape, 1)
    c = jnp.cos(ang)
    s = jnp.sin(ang)
    half = ROT_DIM // 2
    c_ref[...] = jnp.where(lane < ROT_DIM, c, 1.0)
    sa_ref[...] = jnp.where((lane >= half) & (lane < ROT_DIM), s, 0.0)
    sb_ref[...] = jnp.where(lane < half, -s, 0.0)


def _rope_tables(positions, *, tm=1024):
    s = positions.shape[0]
    tm = min(tm, s)
    inv = ROPE_THETA ** (-jnp.arange(0, ROT_DIM, 2, dtype=f32) / ROT_DIM)
    inv_lane = jnp.concatenate(
        [inv, inv, jnp.zeros((HEAD_DIM - ROT_DIM,), f32)]).reshape(1, HEAD_DIM)
    spec = pl.BlockSpec((tm, HEAD_DIM), lambda i: (i, 0))
    shape = jax.ShapeDtypeStruct((s, HEAD_DIM), f32)
    return pl.pallas_call(
        _rope_tables_kernel,
        grid=(s // tm,),
        in_specs=[pl.BlockSpec((tm, 1), lambda i: (i, 0)),
                  pl.BlockSpec((1, HEAD_DIM), lambda i: (0, 0))],
        out_specs=[spec, spec, spec],
        out_shape=[shape, shape, shape],
        compiler_params=_params(1),
        name="rope_tables",
    )(positions.reshape(s, 1), inv_lane)


def _in_proj_kernel(a_ref, w_ref, gq_ref, gk_ref, gm_ref, c_ref, sa_ref, sb_ref, o_ref,
                    *, q_tiles, qk_tiles, v_tiles, mem_head):
    j = pl.program_id(1)
    half = ROT_DIM // 2
    rows = a_ref.shape[0] // 2

    def tile(head, gain, rope):
        chunk = MXU_COLS if head is None else max(head, MXU_COLS)
        for c0 in range(0, o_ref.shape[1], chunk):
            w = w_ref[:, c0:c0 + chunk].astype(bf16)
            for r0 in (0, rows):
                acc = jnp.dot(a_ref[r0:r0 + rows, :], w, preferred_element_type=f32)
                if head is None:
                    o_ref[r0:r0 + rows, c0:c0 + chunk] = acc.astype(o_ref.dtype)
                    continue
                for h0 in range(0, chunk, head):
                    t = acc[:, h0:h0 + head]
                    ms = jnp.mean(t * t, axis=-1, keepdims=True)
                    y = (t * lax.rsqrt(ms + NORM_EPS)) * gain
                    if rope:
                        y = (y * c_ref[r0:r0 + rows, :]
                             + pltpu.roll(y, half, axis=1) * sa_ref[r0:r0 + rows, :]
                             + pltpu.roll(y, head - half, axis=1) * sb_ref[r0:r0 + rows, :])
                    o_ref[r0:r0 + rows, c0 + h0:c0 + h0 + head] = y.astype(o_ref.dtype)

    @pl.when(j < qk_tiles)
    def _():
        tile(HEAD_DIM, jnp.where(j < q_tiles, gq_ref[...], gk_ref[...]), True)

    @pl.when((j >= qk_tiles) & (j < qk_tiles + v_tiles))
    def _():
        tile(None, None, False)

    @pl.when(j >= qk_tiles + v_tiles)
    def _():
        tile(mem_head, gm_ref[...], False)


def _in_proj(a, w, layer, self_width, mem_width, g_q, g_k, g_m, tables, *, tm=1024, tn=1024):
    m, k = a.shape
    n = w.shape[2]
    mem_head = mem_width // MEM_HEADS
    tm = min(tm, m)
    tn = _tile(n, tn, mem_head, math.gcd(self_width, mem_width))
    c, sa, sb = tables
    tab_spec = pl.BlockSpec((tm, HEAD_DIM), lambda i, j: (i, 0))
    g_spec = pl.BlockSpec((1, HEAD_DIM), lambda i, j: (0, 0))
    return pl.pallas_call(
        functools.partial(_in_proj_kernel, q_tiles=self_width // tn, qk_tiles=2 * self_width // tn,
                          v_tiles=self_width // tn, mem_head=mem_head),
        grid=(m // tm, n // tn),
        in_specs=[pl.BlockSpec((tm, k), lambda i, j: (i, 0), pipeline_mode=pl.Buffered(1)),
                  pl.BlockSpec((None, k, tn), lambda i, j: (layer, 0, j)),
                  g_spec, g_spec, pl.BlockSpec((1, mem_head), lambda i, j: (0, 0)),
                  tab_spec, tab_spec, tab_spec],
        out_specs=pl.BlockSpec((tm, tn), lambda i, j: (i, j)),
        out_shape=jax.ShapeDtypeStruct((m, n), bf16),
        compiler_params=_params(2),
        name="in_proj",
    )(a, w, g_q.reshape(1, HEAD_DIM), g_k.reshape(1, HEAD_DIM), g_m.reshape(1, mem_head), c, sa, sb)


def _matmul_kernel(a_ref, w_ref, o_ref):
    o_ref[...] = jnp.dot(a_ref[...], w_ref[...].astype(bf16),
                         preferred_element_type=f32).astype(o_ref.dtype)


def _matmul(a, w, col0, n, out_dtype, *, tm=1024, tn=1024):
    m, k = a.shape
    tm = min(tm, m)
    tn = _tile(n, tn, LANES, col0)
    off = col0 // tn
    return pl.pallas_call(
        _matmul_kernel,
        grid=(m // tm, n // tn),
        in_specs=[pl.BlockSpec((tm, k), lambda i, j: (i, 0)),
                  pl.BlockSpec((k, tn), lambda i, j: (0, j + off))],
        out_specs=pl.BlockSpec((tm, tn), lambda i, j: (i, j)),
        out_shape=jax.ShapeDtypeStruct((m, n), out_dtype),
        compiler_params=_params(2),
        name="matmul",
    )(a, w)


def _outproj_kernel(a1_ref, a2_ref, w1_ref, w2_ref, r_ref, o_ref):
    acc = jnp.dot(a1_ref[...], w1_ref[...].astype(bf16), preferred_element_type=f32)
    acc = acc + jnp.dot(a2_ref[...], w2_ref[...].astype(bf16), preferred_element_type=f32)
    o_ref[...] = r_ref[...] + acc


def _outproj(a1, a2, w, layer, resid, *, tm=1024, tn=512):
    m, k1 = a1.shape
    k2 = a2.shape[1]
    n = w.shape[2]
    tm = min(tm, m)
    tn = _tile(n, tn, LANES)
    assert k1 % k2 == 0
    return pl.pallas_call(
        _outproj_kernel,
        grid=(m // tm, n // tn),
        in_specs=[pl.BlockSpec((tm, k1), lambda i, j: (i, 0)),
                  pl.BlockSpec((tm, k2), lambda i, j: (i, 0)),
                  pl.BlockSpec((None, k1, tn), lambda i, j: (layer, 0, j)),
                  pl.BlockSpec((None, k2, tn), lambda i, j: (layer, k1 // k2, j)),
                  pl.BlockSpec((tm, tn), lambda i, j: (i, j))],
        out_specs=pl.BlockSpec((tm, tn), lambda i, j: (i, j)),
        out_shape=jax.ShapeDtypeStruct((m, n), f32),
        compiler_params=_params(2),
        name="outproj",
    )(a1, a2, w, w, resid)


def _ffn_up_kernel(a_ref, wg_ref, wu_ref, o_ref):
    a = a_ref[...]
    g = jnp.dot(a, wg_ref[...].astype(bf16), preferred_element_type=f32)
    u = jnp.dot(a, wu_ref[...].astype(bf16), preferred_element_type=f32)
    o_ref[...] = ((g * jax.nn.sigmoid(g)) * u).astype(o_ref.dtype)


def _ffn_up(a, wg, wu, layer, *, tm=2048, tn=256):
    m, k = a.shape
    n = wg.shape[2]
    tm = min(tm, m)
    tn = _tile(n, tn, LANES)
    w_spec = pl.BlockSpec((None, k, tn), lambda i, j: (layer, 0, j))
    return pl.pallas_call(
        _ffn_up_kernel,
        grid=(m // tm, n // tn),
        in_specs=[pl.BlockSpec((tm, k), lambda i, j: (i, 0)), w_spec, w_spec],
        out_specs=pl.BlockSpec((tm, tn), lambda i, j: (i, j)),
        out_shape=jax.ShapeDtypeStruct((m, n), bf16),
        compiler_params=_params(2),
        name="ffn_up",
    )(a, wg, wu)


def _ffn_down_kernel(a_ref, w_ref, r_ref, o_ref):
    o_ref[...] = r_ref[...] + jnp.dot(a_ref[...], w_ref[...].astype(bf16),
                                      preferred_element_type=f32)


def _ffn_down(a, w, layer, resid, *, tm=1024, tn=256):
    m, k = a.shape
    n = w.shape[2]
    tm = min(tm, m)
    tn = _tile(n, tn, LANES)
    return pl.pallas_call(
        _ffn_down_kernel,
        grid=(m // tm, n // tn),
        in_specs=[pl.BlockSpec((tm, k), lambda i, j: (i, 0), pipeline_mode=pl.Buffered(1)),
                  pl.BlockSpec((None, k, tn), lambda i, j: (layer, 0, j)),
                  pl.BlockSpec((tm, tn), lambda i, j: (i, j))],
        out_specs=pl.BlockSpec((tm, tn), lambda i, j: (i, j)),
        out_shape=jax.ShapeDtypeStruct((m, n), f32),
        compiler_params=_params(2),
        name="ffn_down",
    )(a, w, resid)


def _transpose_bf16(x):
    return x.astype(f32).T.astype(bf16)


def _flash_streams(streams, i, t_scr, acc_scr, *, scale):
    t, tk = QUERY_TILE, KEY_TILE
    assert t == tk
    c = scale * LOG2_E
    n_streams = len(streams)
    n_past = i
    key = lax.broadcasted_iota(jnp.int32, (tk, t), 0)
    qry = lax.broadcasted_iota(jnp.int32, (tk, t), 1)
    causal = key <= qry

    def stage_scores(n, slot, diagonal):
        raws = [jnp.dot(load_k(n, diagonal), q_rhs, preferred_element_type=f32)
                for q_rhs, load_k, _, _ in streams]
        maxima = []
        for idx, raw in enumerate(raws):
            s = raw * c
            if diagonal:
                s = jnp.where(causal, s, MASK_VALUE)
            t_scr[slot, idx] = s
            maxima.append(jnp.max(s, axis=0, keepdims=True))
        return maxima

    def accumulate(m_old, maxima, slot, v_pair):
        m_out = []
        for idx, (_, _, load_v_t, _) in enumerate(streams):
            m_new = jnp.maximum(m_old[idx], maxima[idx])
            alpha = jnp.exp2(m_old[idx] - m_new)
            p = jnp.exp2(t_scr[slot, idx] - m_new)
            pv = jnp.dot(load_v_t(v_pair), p.astype(bf16), preferred_element_type=f32)
            acc_scr[idx] = alpha * acc_scr[idx] + pv
            m_out.append(m_new)
        return m_out

    maxima = stage_scores(n_past, 0, True)
    m_run = [jnp.full((1, t), -jnp.inf, f32) for _ in streams]
    for idx in range(n_streams):
        acc_scr[idx] = jnp.zeros(acc_scr.shape[1:], f32)

    def step(n, slot, m_run, maxima):
        maxima_next = stage_scores(n, 1 - slot, False)
        m_run = accumulate(m_run, maxima, slot, jnp.where(n == 0, n_past, n - 1))
        return m_run, maxima_next

    def body(trip, carry):
        m_run, maxima = list(carry[:n_streams]), list(carry[n_streams:])
        m_run, maxima = step(2 * trip, 0, m_run, maxima)
        m_run, maxima = step(2 * trip + 1, 1, m_run, maxima)
        return tuple(m_run + maxima)

    carry = lax.fori_loop(0, n_past // 2, body, tuple(m_run + maxima))
    m_run, maxima = list(carry[:n_streams]), list(carry[n_streams:])

    @pl.when(n_past % 2 == 1)
    def _():
        m_mid, maxima_last = step(n_past - 1, 0, m_run, maxima)
        accumulate(m_mid, maxima_last, 1, n_past - 1)

    @pl.when(n_past % 2 == 0)
    def _():
        accumulate(m_run, maxima, 0, jnp.maximum(n_past - 1, 0))

    return [(acc_scr[idx, :dv, :], acc_scr[idx, dv:dv + 1, :])
            for idx, (_, _, _, dv) in enumerate(streams)]


def _fill_v_t(v_ref, v_t_ref, dv):
    t = KEY_TILE
    rows = dv + ONES_ROWS
    for n in range(v_t_ref.shape[0]):
        v_t = _transpose_bf16(v_ref[n * t:(n + 1) * t, :])
        for h in range(v_ref.shape[1] // dv):
            v_t_ref[n, h * rows:h * rows + dv, :] = v_t[h * dv:(h + 1) * dv]
            v_t_ref[n, h * rows + dv:(h + 1) * rows, :] = jnp.ones((ONES_ROWS, t), bf16)


def _moba_kernel(q_ref, k_ref, v_ref, o_ref, v_t_ref, kmean_ref, t_scr, acc_scr, *, n_sel):
    i = pl.program_id(1)
    t, tk, blk_len = QUERY_TILE, KEY_TILE, MOBA_BLOCK
    d = HEAD_DIM
    nb = kmean_ref.shape[0]
    group = q_ref.shape[1] // d
    v_rows = d + ONES_ROWS

    @pl.when(i == 0)
    def _():
        _fill_v_t(v_ref, v_t_ref, d)
        for n in range(nb):
            kb = k_ref[n * blk_len:(n + 1) * blk_len, :].astype(f32)
            kmean_ref[n:n + 1, :] = jnp.sum(kb, axis=0, keepdims=True) / float(blk_len)

    q_t = _transpose_bf16(q_ref[...])
    blk = lax.broadcasted_iota(jnp.int32, (nb, t), 0)
    own = (tk // blk_len) * i + lax.broadcasted_iota(jnp.int32, (nb, t), 1) // blk_len
    past = blk < own
    lane = lax.broadcasted_iota(jnp.int32, (tk, d), 1)
    sub_block = lax.broadcasted_iota(jnp.int32, (tk, d), 0) // blk_len

    def block_onehot(n):
        return jnp.where(lane == (tk // blk_len) * n + sub_block, 1.0, 0.0).astype(bf16)

    streams = []
    for g in range(group):
        rows = slice(g * d, (g + 1) * d)
        gate = jnp.dot(kmean_ref[:, rows].astype(bf16), q_t[rows], preferred_element_type=f32)
        gt = jnp.where(past, gate, -jnp.inf)
        sel = jnp.zeros(gate.shape, jnp.bool_)
        for _ in range(n_sel):
            mx = jnp.max(gt, axis=0, keepdims=True)
            idx = jnp.min(jnp.where(gt == mx, blk, nb), axis=0, keepdims=True)
            pick = blk == idx
            sel = sel | pick
            gt = jnp.where(pick, -jnp.inf, gt)
        mask_rows = jnp.concatenate(
            [jnp.where((sel & past) | (blk == own), 0.0, MASK_VALUE), jnp.zeros((d - nb, t), f32)],
            axis=0).astype(bf16)
        streams.append((
            jnp.concatenate([q_t[rows], mask_rows], axis=0),
            lambda n, diagonal, rows=rows: jnp.concatenate(
                [k_ref[pl.ds(pl.multiple_of(n * tk, tk), tk), rows], block_onehot(n)], axis=1),
            lambda n, g=g: v_t_ref[n, g * v_rows:(g + 1) * v_rows, :],
            d))

    outs = _flash_streams(streams, i, t_scr, acc_scr, scale=d ** -0.5)
    for g, (acc, l) in enumerate(outs):
        o_ref[:, g * d:(g + 1) * d] = (acc * (1.0 / l)).T.astype(o_ref.dtype)


def _moba_attention(proj, n_heads, *, group=4):
    s = proj.shape[0]
    t = QUERY_TILE
    assert t % MOBA_BLOCK == 0 and s % t == 0 and n_heads % group == 0
    nb = s // MOBA_BLOCK
    assert nb <= HEAD_DIM
    v_rows = HEAD_DIM + ONES_ROWS
    n_sel = min(MOBA_TOPK, max(nb - 1, 1))
    w = group * HEAD_DIM
    n_groups = n_heads // group
    return pl.pallas_call(
        functools.partial(_moba_kernel, n_sel=n_sel),
        grid=(n_groups, s // t),
        in_specs=[pl.BlockSpec((t, w), lambda h, i: (i, h)),
                  pl.BlockSpec((s, w), lambda h, i: (0, n_groups + h)),
                  pl.BlockSpec((s, w), lambda h, i: (0, 2 * n_groups + h))],
        out_specs=pl.BlockSpec((t, w), lambda h, i: (i, h)),
        out_shape=jax.ShapeDtypeStruct((s, n_heads * HEAD_DIM), bf16),
        scratch_shapes=[pltpu.VMEM((s // KEY_TILE, group * v_rows, KEY_TILE), bf16),
                        pltpu.VMEM((nb, w), f32),
                        pltpu.VMEM((2, group, KEY_TILE, t), f32),
                        pltpu.VMEM((group, v_rows, t), f32)],
        compiler_params=_params(2, INTERLEAVE_FLAGS),
        name="moba_attention",
    )(proj, proj, proj)


def _diff_kernel(q_ref, k_ref, v_ref, lq1_ref, lk1_ref, lq2_ref, lk2_ref, g_ref, o_ref, v_t_ref,
                 t_scr, acc_scr, *, lam_init):
    i = pl.program_id(1)
    tk = KEY_TILE
    d = HEAD_DIM
    group = q_ref.shape[1] // (2 * d)

    v_rows = 2 * d + ONES_ROWS

    @pl.when(i == 0)
    def _():
        _fill_v_t(v_ref, v_t_ref, 2 * d)

    lam = (jnp.exp(jnp.sum(lq1_ref[...] * lk1_ref[...], axis=-1, keepdims=True))
           - jnp.exp(jnp.sum(lq2_ref[...] * lk2_ref[...], axis=-1, keepdims=True))
           + lam_init)
    q_t = _transpose_bf16(q_ref[...])
    streams = []
    for a in range(group):
        vrows = slice(a * v_rows, (a + 1) * v_rows)
        for sub in range(2):
            rows = slice((2 * a + sub) * d, (2 * a + sub + 1) * d)
            streams.append((
                q_t[rows],
                lambda n, diagonal, rows=rows: k_ref[pl.ds(pl.multiple_of(n * tk, tk), tk), rows],
                lambda n, vrows=vrows: v_t_ref[n, vrows, :],
                2 * d))
    outs = _flash_streams(streams, i, t_scr, acc_scr, scale=d ** -0.5)
    for a in range(group):
        (acc1, l1), (acc2, l2) = outs[2 * a], outs[2 * a + 1]
        o = acc1 * (1.0 / l1) - lam * (acc2 * (1.0 / l2))
        ms = jnp.mean(o * o, axis=0, keepdims=True)
        y = (o * lax.rsqrt(ms + SUBLN_EPS)).T * g_ref[...]
        o_ref[:, a * 2 * d:(a + 1) * 2 * d] = (y * (1.0 - lam_init)).astype(o_ref.dtype)


def _diff_attention(proj, n_heads, lq1, lk1, lq2, lk2, g_subln, lam_init, *, group=2):
    s = proj.shape[0]
    t = QUERY_TILE
    assert s % t == 0 and n_heads % group == 0
    d2 = 2 * HEAD_DIM
    w = group * d2
    n_groups = n_heads // group
    vec = pl.BlockSpec((1, HEAD_DIM), lambda h, i: (0, 0))
    return pl.pallas_call(
        functools.partial(_diff_kernel, lam_init=lam_init),
        grid=(n_groups, s // t),
        in_specs=[pl.BlockSpec((t, w), lambda h, i: (i, h)),
                  pl.BlockSpec((s, w), lambda h, i: (0, n_groups + h)),
                  pl.BlockSpec((s, w), lambda h, i: (0, 2 * n_groups + h)),
                  vec, vec, vec, vec,
                  pl.BlockSpec((1, d2), lambda h, i: (0, 0))],
        out_specs=pl.BlockSpec((t, w), lambda h, i: (i, h)),
        out_shape=jax.ShapeDtypeStruct((s, n_heads * d2), bf16),
        scratch_shapes=[pltpu.VMEM((s // KEY_TILE, group * (d2 + ONES_ROWS), KEY_TILE), bf16),
                        pltpu.VMEM((2, 2 * group, KEY_TILE, t), f32),
                        pltpu.VMEM((2 * group, d2 + ONES_ROWS, t), f32)],
        compiler_params=_params(2, INTERLEAVE_FLAGS),
        name="diff_attention",
    )(proj, proj, proj, lq1.reshape(1, -1), lk1.reshape(1, -1), lq2.reshape(1, -1), lk2.reshape(1, -1),
      g_subln.reshape(1, d2))


def _mem_attn_kernel(q_ref, mk_ref, mv_ref, g_ref, o_ref, *, n_heads):
    dm = q_ref.shape[1] // n_heads
    scale = dm ** -0.5
    for h in range(n_heads):
        cols = slice(h * dm, (h + 1) * dm)
        kraw = mk_ref[:, cols]
        ms = jnp.mean(kraw * kraw, axis=-1, keepdims=True)
        kn = ((kraw * lax.rsqrt(ms + NORM_EPS)) * g_ref[...]).astype(bf16)
        s = lax.dot_general(q_ref[:, cols], kn, (((1,), (1,)), ((), ())),
                            preferred_element_type=f32) * scale
        m = jnp.max(s, axis=-1, keepdims=True)
        p = jnp.exp(s - m)
        l = jnp.sum(p, axis=-1, keepdims=True)
        o = jnp.dot(p.astype(bf16), mv_ref[:, cols].astype(bf16), preferred_element_type=f32)
        o_ref[:, cols] = (o * (1.0 / l)).astype(o_ref.dtype)


def _mem_attention(proj, mkv, g_knorm, *, tq=512):
    s = proj.shape[0]
    mlen, wm = mkv.shape[0], mkv.shape[1] // 2
    q_block = proj.shape[1] // wm - 1
    tq = min(tq, s)
    dm = wm // MEM_HEADS
    return pl.pallas_call(
        functools.partial(_mem_attn_kernel, n_heads=MEM_HEADS),
        grid=(s // tq,),
        in_specs=[pl.BlockSpec((tq, wm), lambda i: (i, q_block)),
                  pl.BlockSpec((mlen, wm), lambda i: (0, 0)),
                  pl.BlockSpec((mlen, wm), lambda i: (0, 1)),
                  pl.BlockSpec((1, dm), lambda i: (0, 0))],
        out_specs=pl.BlockSpec((tq, wm), lambda i: (i, 0)),
        out_shape=jax.ShapeDtypeStruct((s, wm), bf16),
        compiler_params=_params(1),
        name="mem_attention",
    )(proj, mkv, mkv, g_knorm.reshape(1, dm))


def kernel(x, mem, positions, g_attn_norm, w_in, w_out, g_qnorm, g_knorm, g_mem_qnorm, g_mem_knorm, g_mem_norm, w_mem_kv, lambda_q1, lambda_k1, lambda_q2, lambda_k2, g_subln, g_ffn_norm, w_gate, w_up, w_down):
    b, s, d_model = x.shape
    assert b == 1
    depth = w_in.shape[0]
    mem_width = w_mem_kv.shape[1] // 2
    self_width = w_out.shape[1] - mem_width
    n_heads = self_width // HEAD_DIM
    mem_head = mem_width // MEM_HEADS

    xs = x.reshape(s, d_model)
    tables = _rope_tables(positions.reshape(s))
    mem_n = _rmsnorm(mem.reshape(mem.shape[1], d_model), g_mem_norm)
    mkv = _matmul(mem_n, w_mem_kv, 0, 2 * mem_width, f32)

    for i in range(depth):
        h = _rmsnorm(xs, g_attn_norm[i])
        proj = _in_proj(h, w_in, i, self_width, mem_width, g_qnorm[i], g_knorm[i], g_mem_qnorm[i],
                        tables)
        if i % 2 == 0:
            self_out = _moba_attention(proj, n_heads)
        else:
            j = i // 2
            lam_init = 0.8 - 0.6 * math.exp(-0.3 * i)
            self_out = _diff_attention(proj, n_heads // 2, lambda_q1[j], lambda_k1[j],
                                       lambda_q2[j], lambda_k2[j], g_subln[j], lam_init)
        mem_out = _mem_attention(proj, mkv, g_mem_knorm[i])
        xs = _outproj(self_out, mem_out, w_out, i, xs)
        f = _rmsnorm(xs, g_ffn_norm[i])
        act = _ffn_up(f, w_gate, w_up, i)
        xs = _ffn_down(act, w_down, i, xs)
    return xs.reshape(b, s, d_model)
```

```python
import functools
import math

import jax
import jax.numpy as jnp
from jax import lax
from jax.experimental import pallas as pl
from jax.experimental.pallas import tpu as pltpu

f32 = jnp.float32
bf16 = jnp.bfloat16

HEAD_DIM = 128
MEM_HEADS = 4
ROT_DIM = HEAD_DIM // 4
ROPE_THETA = 500000.0
MOBA_BLOCK = 256
MOBA_TOPK = 3
NORM_EPS = 1e-6
SUBLN_EPS = 1e-5
LOG2_E = 1.4426950408889634
MASK_VALUE = -1e30

LANES = 128
MXU_COLS = 256
ONES_ROWS = 16
ROW_SPLIT = 4
ATTN_TILE = 256
KEY_TILE = 2 * ATTN_TILE
QUERY_TILE = KEY_TILE
VMEM_LIMIT = 56 * 1024 * 1024


def _tile(n, target, align, col0=0):
    t = (min(target, n) // align) * align
    while n % t or col0 % t:
        t -= align
    return t


def _params(n_grid, flags=None):
    return pltpu.CompilerParams(
        dimension_semantics=("arbitrary",) * n_grid, vmem_limit_bytes=VMEM_LIMIT, flags=flags)


INTERLEAVE_FLAGS = None


def _rmsnorm_kernel(x_ref, g_ref, o_ref, *, eps):
    x = x_ref[...]
    ms = jnp.mean(x * x, axis=-1, keepdims=True)
    o_ref[...] = ((x * lax.rsqrt(ms + eps)) * g_ref[...]).astype(o_ref.dtype)


def _rmsnorm(x, g, *, eps=NORM_EPS, tm=256):
    m, d = x.shape
    tm = min(tm, m)
    return pl.pallas_call(
        functools.partial(_rmsnorm_kernel, eps=eps),
        grid=(m // tm,),
        in_specs=[pl.BlockSpec((tm, d), lambda i: (i, 0)),
                  pl.BlockSpec((1, d), lambda i: (0, 0))],
        out_specs=pl.BlockSpec((tm, d), lambda i: (i, 0)),
        out_shape=jax.ShapeDtypeStruct((m, d), bf16),
        compiler_params=_params(1),
        name="rmsnorm",
    )(x, g.reshape(1, d))


def _rope_tables_kernel(pos_ref, inv_ref, c_ref, sa_ref, sb_ref):
    ang = pos_ref[...].astype(f32) * inv_ref[...]
    lane = lax.broadcasted_iota(jnp.int32, ang.shape, 1)
    c = jnp.cos(ang)
    s = jnp.sin(ang)
    half = ROT_DIM // 2
    c_ref[...] = jnp.where(lane < ROT_DIM, c, 1.0)
    sa_ref[...] = jnp.where((lane >= half) & (lane < ROT_DIM), s, 0.0)
    sb_ref[...] = jnp.where(lane < half, -s, 0.0)


def _rope_tables(positions, *, tm=1024):
    s = positions.shape[0]
    tm = min(tm, s)
    inv = ROPE_THETA ** (-jnp.arange(0, ROT_DIM, 2, dtype=f32) / ROT_DIM)
    inv_lane = jnp.concatenate(
        [inv, inv, jnp.zeros((HEAD_DIM - ROT_DIM,), f32)]).reshape(1, HEAD_DIM)
    spec = pl.BlockSpec((tm, HEAD_DIM), lambda i: (i, 0))
    shape = jax.ShapeDtypeStruct((s, HEAD_DIM), f32)
    return pl.pallas_call(
        _rope_tables_kernel,
        grid=(s // tm,),
        in_specs=[pl.BlockSpec((tm, 1), lambda i: (i, 0)),
                  pl.BlockSpec((1, HEAD_DIM), lambda i: (0, 0))],
        out_specs=[spec, spec, spec],
        out_shape=[shape, shape, shape],
        compiler_params=_params(1),
        name="rope_tables",
    )(positions.reshape(s, 1), inv_lane)


def _in_proj_kernel(a_ref, w_ref, gq_ref, gk_ref, gm_ref, c_ref, sa_ref, sb_ref, o_ref,
                    *, q_tiles, qk_tiles, v_tiles, mem_head):
    j = pl.program_id(1)
    half = ROT_DIM // 2
    rows = a_ref.shape[0] // ROW_SPLIT

    def tile(head, gain, rope):
        chunk = MXU_COLS if head is None else max(head, MXU_COLS)
        for c0 in range(0, o_ref.shape[1], chunk):
            w = w_ref[:, c0:c0 + chunk].astype(bf16)
            for r0 in range(0, a_ref.shape[0], rows):
                acc = jnp.dot(a_ref[r0:r0 + rows, :], w, preferred_element_type=f32)
                if head is None:
                    o_ref[r0:r0 + rows, c0:c0 + chunk] = acc.astype(o_ref.dtype)
                    continue
                for h0 in range(0, chunk, head):
                    t = acc[:, h0:h0 + head]
                    ms = jnp.mean(t * t, axis=-1, keepdims=True)
                    y = (t * lax.rsqrt(ms + NORM_EPS)) * gain
                    if rope:
                        y = (y * c_ref[r0:r0 + rows, :]
                             + pltpu.roll(y, half, axis=1) * sa_ref[r0:r0 + rows, :]
                             + pltpu.roll(y, head - half, axis=1) * sb_ref[r0:r0 + rows, :])
                    o_ref[r0:r0 + rows, c0 + h0:c0 + h0 + head] = y.astype(o_ref.dtype)

    @pl.when(j < qk_tiles)
    def _():
        tile(HEAD_DIM, jnp.where(j < q_tiles, gq_ref[...], gk_ref[...]), True)

    @pl.when((j >= qk_tiles) & (j < qk_tiles + v_tiles))
    def _():
        tile(None, None, False)

    @pl.when(j >= qk_tiles + v_tiles)
    def _():
        tile(mem_head, gm_ref[...], False)


def _in_proj(a, w, layer, self_width, mem_width, g_q, g_k, g_m, tables, *, tm=1024, tn=1024):
    m, k = a.shape
    n = w.shape[2]
    mem_head = mem_width // MEM_HEADS
    tm = min(tm, m)
    tn = _tile(n, tn, mem_head, math.gcd(self_width, mem_width))
    c, sa, sb = tables
    tab_spec = pl.BlockSpec((tm, HEAD_DIM), lambda i, j: (i, 0))
    g_spec = pl.BlockSpec((1, HEAD_DIM), lambda i, j: (0, 0))
    return pl.pallas_call(
        functools.partial(_in_proj_kernel, q_tiles=self_width // tn, qk_tiles=2 * self_width // tn,
                          v_tiles=self_width // tn, mem_head=mem_head),
        grid=(m // tm, n // tn),
        in_specs=[pl.BlockSpec((tm, k), lambda i, j: (i, 0), pipeline_mode=pl.Buffered(1)),
                  pl.BlockSpec((None, k, tn), lambda i, j: (layer, 0, j)),
                  g_spec, g_spec, pl.BlockSpec((1, mem_head), lambda i, j: (0, 0)),
                  tab_spec, tab_spec, tab_spec],
        out_specs=pl.BlockSpec((tm, tn), lambda i, j: (i, j)),
        out_shape=jax.ShapeDtypeStruct((m, n), bf16),
        compiler_params=_params(2),
        name="in_proj",
    )(a, w, g_q.reshape(1, HEAD_DIM), g_k.reshape(1, HEAD_DIM), g_m.reshape(1, mem_head), c, sa, sb)


def _matmul_kernel(a_ref, w_ref, o_ref):
    o_ref[...] = jnp.dot(a_ref[...], w_ref[...].astype(bf16),
                         preferred_element_type=f32).astype(o_ref.dtype)


def _matmul(a, w, col0, n, out_dtype, *, tm=1024, tn=1024):
    m, k = a.shape
    tm = min(tm, m)
    tn = _tile(n, tn, LANES, col0)
    off = col0 // tn
    return pl.pallas_call(
        _matmul_kernel,
        grid=(m // tm, n // tn),
        in_specs=[pl.BlockSpec((tm, k), lambda i, j: (i, 0)),
                  pl.BlockSpec((k, tn), lambda i, j: (0, j + off))],
        out_specs=pl.BlockSpec((tm, tn), lambda i, j: (i, j)),
        out_shape=jax.ShapeDtypeStruct((m, n), out_dtype),
        compiler_params=_params(2),
        name="matmul",
    )(a, w)


def _outproj_kernel(a1_ref, a2_ref, w1_ref, w2_ref, r_ref, o_ref):
    acc = jnp.dot(a1_ref[...], w1_ref[...].astype(bf16), preferred_element_type=f32)
    acc = acc + jnp.dot(a2_ref[...], w2_ref[...].astype(bf16), preferred_element_type=f32)
    o_ref[...] = r_ref[...] + acc


def _outproj(a1, a2, w, layer, resid, *, tm=1024, tn=512):
    m, k1 = a1.shape
    k2 = a2.shape[1]
    n = w.shape[2]
    tm = min(tm, m)
    tn = _tile(n, tn, LANES)
    assert k1 % k2 == 0
    return pl.pallas_call(
        _outproj_kernel,
        grid=(m // tm, n // tn),
        in_specs=[pl.BlockSpec((tm, k1), lambda i, j: (i, 0)),
                  pl.BlockSpec((tm, k2), lambda i, j: (i, 0)),
                  pl.BlockSpec((None, k1, tn), lambda i, j: (layer, 0, j)),
                  pl.BlockSpec((None, k2, tn), lambda i, j: (layer, k1 // k2, j)),
                  pl.BlockSpec((tm, tn), lambda i, j: (i, j))],
        out_specs=pl.BlockSpec((tm, tn), lambda i, j: (i, j)),
        out_shape=jax.ShapeDtypeStruct((m, n), f32),
        compiler_params=_params(2),
        name="outproj",
    )(a1, a2, w, w, resid)


def _ffn_up_kernel(a_ref, wg_ref, wu_ref, o_ref):
    a = a_ref[...]
    g = jnp.dot(a, wg_ref[...].astype(bf16), preferred_element_type=f32)
    u = jnp.dot(a, wu_ref[...].astype(bf16), preferred_element_type=f32)
    o_ref[...] = ((g * jax.nn.sigmoid(g)) * u).astype(o_ref.dtype)


def _ffn_up(a, wg, wu, layer, *, tm=2048, tn=256):
    m, k = a.shape
    n = wg.shape[2]
    tm = min(tm, m)
    tn = _tile(n, tn, LANES)
    w_spec = pl.BlockSpec((None, k, tn), lambda i, j: (layer, 0, j))
    return pl.pallas_call(
        _ffn_up_kernel,
        grid=(m // tm, n // tn),
        in_specs=[pl.BlockSpec((tm, k), lambda i, j: (i, 0)), w_spec, w_spec],
        out_specs=pl.BlockSpec((tm, tn), lambda i, j: (i, j)),
        out_shape=jax.ShapeDtypeStruct((m, n), bf16),
        compiler_params=_params(2),
        name="ffn_up",
    )(a, wg, wu)


def _ffn_down_kernel(a_ref, w_ref, r_ref, o_ref):
    o_ref[...] = r_ref[...] + jnp.dot(a_ref[...], w_ref[...].astype(bf16),
                                      preferred_element_type=f32)


def _ffn_down(a, w, layer, resid, *, tm=1024, tn=256):
    m, k = a.shape
    n = w.shape[2]
    tm = min(tm, m)
    tn = _tile(n, tn, LANES)
    return pl.pallas_call(
        _ffn_down_kernel,
        grid=(m // tm, n // tn),
        in_specs=[pl.BlockSpec((tm, k), lambda i, j: (i, 0), pipeline_mode=pl.Buffered(1)),
                  pl.BlockSpec((None, k, tn), lambda i, j: (layer, 0, j)),
                  pl.BlockSpec((tm, tn), lambda i, j: (i, j))],
        out_specs=pl.BlockSpec((tm, tn), lambda i, j: (i, j)),
        out_shape=jax.ShapeDtypeStruct((m, n), f32),
        compiler_params=_params(2),
        name="ffn_down",
    )(a, w, resid)


def _transpose_bf16(x):
    return x.astype(f32).T.astype(bf16)


def _flash_streams(streams, i, t_scr, acc_scr, *, scale):
    t, tk = QUERY_TILE, KEY_TILE
    assert t == tk
    c = scale * LOG2_E
    n_streams = len(streams)
    n_past = i
    key = lax.broadcasted_iota(jnp.int32, (tk, t), 0)
    qry = lax.broadcasted_iota(jnp.int32, (tk, t), 1)
    causal = key <= qry

    def stage_scores(n, slot, diagonal):
        raws = [jnp.dot(load_k(n, diagonal), q_rhs, preferred_element_type=f32)
                for q_rhs, load_k, _, _ in streams]
        maxima = []
        for idx, raw in enumerate(raws):
            s = raw * c
            if diagonal:
                s = jnp.where(causal, s, MASK_VALUE)
            t_scr[slot, idx] = s
            maxima.append(jnp.max(s, axis=0, keepdims=True))
        return maxima

    def accumulate(m_old, maxima, slot, v_pair):
        m_out = []
        for idx, (_, _, load_v_t, _) in enumerate(streams):
            m_new = jnp.maximum(m_old[idx], maxima[idx])
            alpha = jnp.exp2(m_old[idx] - m_new)
            p = jnp.exp2(t_scr[slot, idx] - m_new)
            pv = jnp.dot(load_v_t(v_pair), p.astype(bf16), preferred_element_type=f32)
            acc_scr[idx] = alpha * acc_scr[idx] + pv
            m_out.append(m_new)
        return m_out

    maxima = stage_scores(n_past, 0, True)
    m_run = [jnp.full((1, t), -jnp.inf, f32) for _ in streams]
    for idx in range(n_streams):
        acc_scr[idx] = jnp.zeros(acc_scr.shape[1:], f32)

    def step(n, slot, m_run, maxima):
        maxima_next = stage_scores(n, 1 - slot, False)
        m_run = accumulate(m_run, maxima, slot, jnp.where(n == 0, n_past, n - 1))
        return m_run, maxima_next

    def body(trip, carry):
        m_run, maxima = list(carry[:n_streams]), list(carry[n_streams:])
        m_run, maxima = step(2 * trip, 0, m_run, maxima)
        m_run, maxima = step(2 * trip + 1, 1, m_run, maxima)
        return tuple(m_run + maxima)

    carry = lax.fori_loop(0, n_past // 2, body, tuple(m_run + maxima))
    m_run, maxima = list(carry[:n_streams]), list(carry[n_streams:])

    @pl.when(n_past % 2 == 1)
    def _():
        m_mid, maxima_last = step(n_past - 1, 0, m_run, maxima)
        accumulate(m_mid, maxima_last, 1, n_past - 1)

    @pl.when(n_past % 2 == 0)
    def _():
        accumulate(m_run, maxima, 0, jnp.maximum(n_past - 1, 0))

    return [(acc_scr[idx, :dv, :], acc_scr[idx, dv:dv + 1, :])
            for idx, (_, _, _, dv) in enumerate(streams)]


def _fill_v_t(v_ref, v_t_ref, dv):
    t = KEY_TILE
    rows = dv + ONES_ROWS
    for n in range(v_t_ref.shape[0]):
        v_t = _transpose_bf16(v_ref[n * t:(n + 1) * t, :])
        for h in range(v_ref.shape[1] // dv):
            v_t_ref[n, h * rows:h * rows + dv, :] = v_t[h * dv:(h + 1) * dv]
            v_t_ref[n, h * rows + dv:(h + 1) * rows, :] = jnp.ones((ONES_ROWS, t), bf16)


def _moba_kernel(q_ref, k_ref, v_ref, o_ref, v_t_ref, kmean_ref, t_scr, acc_scr, *, n_sel):
    i = pl.program_id(1)
    t, tk, blk_len = QUERY_TILE, KEY_TILE, MOBA_BLOCK
    d = HEAD_DIM
    nb = kmean_ref.shape[0]
    group = q_ref.shape[1] // d
    v_rows = d + ONES_ROWS

    @pl.when(i == 0)
    def _():
        _fill_v_t(v_ref, v_t_ref, d)
        for n in range(nb):
            kb = k_ref[n * blk_len:(n + 1) * blk_len, :].astype(f32)
            kmean_ref[n:n + 1, :] = jnp.sum(kb, axis=0, keepdims=True) / float(blk_len)

    q_t = _transpose_bf16(q_ref[...])
    blk = lax.broadcasted_iota(jnp.int32, (nb, t), 0)
    own = (tk // blk_len) * i + lax.broadcasted_iota(jnp.int32, (nb, t), 1) // blk_len
    past = blk < own
    lane = lax.broadcasted_iota(jnp.int32, (tk, d), 1)
    sub_block = lax.broadcasted_iota(jnp.int32, (tk, d), 0) // blk_len

    def block_onehot(n):
        return jnp.where(lane == (tk // blk_len) * n + sub_block, 1.0, 0.0).astype(bf16)

    streams = []
    for g in range(group):
        rows = slice(g * d, (g + 1) * d)
        gate = jnp.dot(kmean_ref[:, rows].astype(bf16), q_t[rows], preferred_element_type=f32)
        gt = jnp.where(past, gate, -jnp.inf)
        sel = jnp.zeros(gate.shape, jnp.bool_)
        for _ in range(n_sel):
            mx = jnp.max(gt, axis=0, keepdims=True)
            idx = jnp.min(jnp.where(gt == mx, blk, nb), axis=0, keepdims=True)
            pick = blk == idx
            sel = sel | pick
            gt = jnp.where(pick, -jnp.inf, gt)
        mask_rows = jnp.concatenate(
            [jnp.where((sel & past) | (blk == own), 0.0, MASK_VALUE), jnp.zeros((d - nb, t), f32)],
            axis=0).astype(bf16)
        streams.append((
            jnp.concatenate([q_t[rows], mask_rows], axis=0),
            lambda n, diagonal, rows=rows: jnp.concatenate(
                [k_ref[pl.ds(pl.multiple_of(n * tk, tk), tk), rows], block_onehot(n)], axis=1),
            lambda n, g=g: v_t_ref[n, g * v_rows:(g + 1) * v_rows, :],
            d))

    outs = _flash_streams(streams, i, t_scr, acc_scr, scale=d ** -0.5)
    for g, (acc, l) in enumerate(outs):
        o_ref[:, g * d:(g + 1) * d] = (acc * (1.0 / l)).T.astype(o_ref.dtype)


def _moba_attention(proj, n_heads, *, group=4):
    s = proj.shape[0]
    t = QUERY_TILE
    assert t % MOBA_BLOCK == 0 and s % t == 0 and n_heads % group == 0
    nb = s // MOBA_BLOCK
    assert nb <= HEAD_DIM
    v_rows = HEAD_DIM + ONES_ROWS
    n_sel = min(MOBA_TOPK, max(nb - 1, 1))
    w = group * HEAD_DIM
    n_groups = n_heads // group
    return pl.pallas_call(
        functools.partial(_moba_kernel, n_sel=n_sel),
        grid=(n_groups, s // t),
        in_specs=[pl.BlockSpec((t, w), lambda h, i: (i, h)),
                  pl.BlockSpec((s, w), lambda h, i: (0, n_groups + h)),
                  pl.BlockSpec((s, w), lambda h, i: (0, 2 * n_groups + h))],
        out_specs=pl.BlockSpec((t, w), lambda h, i: (i, h)),
        out_shape=jax.ShapeDtypeStruct((s, n_heads * HEAD_DIM), bf16),
        scratch_shapes=[pltpu.VMEM((s // KEY_TILE, group * v_rows, KEY_TILE), bf16),
                        pltpu.VMEM((nb, w), f32),
                        pltpu.VMEM((2, group, KEY_TILE, t), f32),
                        pltpu.VMEM((group, v_rows, t), f32)],
        compiler_params=_params(2, INTERLEAVE_FLAGS),
        name="moba_attention",
    )(proj, proj, proj)


def _diff_kernel(q_ref, k_ref, v_ref, lq1_ref, lk1_ref, lq2_ref, lk2_ref, g_ref, o_ref, v_t_ref,
                 t_scr, acc_scr, *, lam_init):
    i = pl.program_id(1)
    tk = KEY_TILE
    d = HEAD_DIM
    group = q_ref.shape[1] // (2 * d)

    v_rows = 2 * d + ONES_ROWS

    @pl.when(i == 0)
    def _():
        _fill_v_t(v_ref, v_t_ref, 2 * d)

    lam = (jnp.exp(jnp.sum(lq1_ref[...] * lk1_ref[...], axis=-1, keepdims=True))
           - jnp.exp(jnp.sum(lq2_ref[...] * lk2_ref[...], axis=-1, keepdims=True))
           + lam_init)
    q_t = _transpose_bf16(q_ref[...])
    streams = []
    for a in range(group):
        vrows = slice(a * v_rows, (a + 1) * v_rows)
        for sub in range(2):
            rows = slice((2 * a + sub) * d, (2 * a + sub + 1) * d)
            streams.append((
                q_t[rows],
                lambda n, diagonal, rows=rows: k_ref[pl.ds(pl.multiple_of(n * tk, tk), tk), rows],
                lambda n, vrows=vrows: v_t_ref[n, vrows, :],
                2 * d))
    outs = _flash_streams(streams, i, t_scr, acc_scr, scale=d ** -0.5)
    for a in range(group):
        (acc1, l1), (acc2, l2) = outs[2 * a], outs[2 * a + 1]
        o = acc1 * (1.0 / l1) - lam * (acc2 * (1.0 / l2))
        ms = jnp.mean(o * o, axis=0, keepdims=True)
        y = (o * lax.rsqrt(ms + SUBLN_EPS)).T * g_ref[...]
        o_ref[:, a * 2 * d:(a + 1) * 2 * d] = (y * (1.0 - lam_init)).astype(o_ref.dtype)


def _diff_attention(proj, n_heads, lq1, lk1, lq2, lk2, g_subln, lam_init, *, group=2):
    s = proj.shape[0]
    t = QUERY_TILE
    assert s % t == 0 and n_heads % group == 0
    d2 = 2 * HEAD_DIM
    w = group * d2
    n_groups = n_heads // group
    vec = pl.BlockSpec((1, HEAD_DIM), lambda h, i: (0, 0))
    return pl.pallas_call(
        functools.partial(_diff_kernel, lam_init=lam_init),
        grid=(n_groups, s // t),
        in_specs=[pl.BlockSpec((t, w), lambda h, i: (i, h)),
                  pl.BlockSpec((s, w), lambda h, i: (0, n_groups + h)),
                  pl.BlockSpec((s, w), lambda h, i: (0, 2 * n_groups + h)),
                  vec, vec, vec, vec,
                  pl.BlockSpec((1, d2), lambda h, i: (0, 0))],
        out_specs=pl.BlockSpec((t, w), lambda h, i: (i, h)),
        out_shape=jax.ShapeDtypeStruct((s, n_heads * d2), bf16),
        scratch_shapes=[pltpu.VMEM((s // KEY_TILE, group * (d2 + ONES_ROWS), KEY_TILE), bf16),
                        pltpu.VMEM((2, 2 * group, KEY_TILE, t), f32),
                        pltpu.VMEM((2 * group, d2 + ONES_ROWS, t), f32)],
        compiler_params=_params(2, INTERLEAVE_FLAGS),
        name="diff_attention",
    )(proj, proj, proj, lq1.reshape(1, -1), lk1.reshape(1, -1), lq2.reshape(1, -1), lk2.reshape(1, -1),
      g_subln.reshape(1, d2))


def _mem_attn_kernel(q_ref, mk_ref, mv_ref, g_ref, o_ref, *, n_heads):
    dm = q_ref.shape[1] // n_heads
    scale = dm ** -0.5
    for h in range(n_heads):
        cols = slice(h * dm, (h + 1) * dm)
        kraw = mk_ref[:, cols]
        ms = jnp.mean(kraw * kraw, axis=-1, keepdims=True)
        kn = ((kraw * lax.rsqrt(ms + NORM_EPS)) * g_ref[...]).astype(bf16)
        s = lax.dot_general(q_ref[:, cols], kn, (((1,), (1,)), ((), ())),
                            preferred_element_type=f32) * scale
        m = jnp.max(s, axis=-1, keepdims=True)
        p = jnp.exp(s - m)
        l = jnp.sum(p, axis=-1, keepdims=True)
        o = jnp.dot(p.astype(bf16), mv_ref[:, cols].astype(bf16), preferred_element_type=f32)
        o_ref[:, cols] = (o * (1.0 / l)).astype(o_ref.dtype)


def _mem_attention(proj, mkv, g_knorm, *, tq=512):
    s = proj.shape[0]
    mlen, wm = mkv.shape[0], mkv.shape[1] // 2
    q_block = proj.shape[1] // wm - 1
    tq = min(tq, s)
    dm = wm // MEM_HEADS
    return pl.pallas_call(
        functools.partial(_mem_attn_kernel, n_heads=MEM_HEADS),
        grid=(s // tq,),
        in_specs=[pl.BlockSpec((tq, wm), lambda i: (i, q_block)),
                  pl.BlockSpec((mlen, wm), lambda i: (0, 0)),
                  pl.BlockSpec((mlen, wm), lambda i: (0, 1)),
                  pl.BlockSpec((1, dm), lambda i: (0, 0))],
        out_specs=pl.BlockSpec((tq, wm), lambda i: (i, 0)),
        out_shape=jax.ShapeDtypeStruct((s, wm), bf16),
        compiler_params=_params(1),
        name="mem_attention",
    )(proj, mkv, mkv, g_knorm.reshape(1, dm))


def kernel(x, mem, positions, g_attn_norm, w_in, w_out, g_qnorm, g_knorm, g_mem_qnorm, g_mem_knorm, g_mem_norm, w_mem_kv, lambda_q1, lambda_k1, lambda_q2, lambda_k2, g_subln, g_ffn_norm, w_gate, w_up, w_down):
    b, s, d_model = x.shape
    assert b == 1
    depth = w_in.shape[0]
    mem_width = w_mem_kv.shape[1] // 2
    self_width = w_out.shape[1] - mem_width
    n_heads = self_width // HEAD_DIM
    mem_head = mem_width // MEM_HEADS

    xs = x.reshape(s, d_model)
    tables = _rope_tables(positions.reshape(s))
    mem_n = _rmsnorm(mem.reshape(mem.shape[1], d_model), g_mem_norm)
    mkv = _matmul(mem_n, w_mem_kv, 0, 2 * mem_width, f32)

    for i in range(depth):
        h = _rmsnorm(xs, g_attn_norm[i])
        proj = _in_proj(h, w_in, i, self_width, mem_width, g_qnorm[i], g_knorm[i], g_mem_qnorm[i],
                        tables)
        if i % 2 == 0:
            self_out = _moba_attention(proj, n_heads)
        else:
            j = i // 2
            lam_init = 0.8 - 0.6 * math.exp(-0.3 * i)
            self_out = _diff_attention(proj, n_heads // 2, lambda_q1[j], lambda_k1[j],
                                       lambda_q2[j], lambda_k2[j], g_subln[j], lam_init)
        mem_out = _mem_attention(proj, mkv, g_mem_knorm[i])
        xs = _outproj(self_out, mem_out, w_out, i, xs)
        f = _rmsnorm(xs, g_ffn_norm[i])
        act = _ffn_up(f, w_gate, w_up, i)
        xs = _ffn_down(act, w_down, i, xs)
    return xs.reshape(b, s, d_model)
```

```python
import functools
import math

import jax
import jax.numpy as jnp
from jax import lax
from jax.experimental import pallas as pl
from jax.experimental.pallas import tpu as pltpu

f32 = jnp.float32
bf16 = jnp.bfloat16

HEAD_DIM = 128
MEM_HEADS = 4
ROT_DIM = HEAD_DIM // 4
ROPE_THETA = 500000.0
MOBA_BLOCK = 256
MOBA_TOPK = 3
NORM_EPS = 1e-6
SUBLN_EPS = 1e-5
LOG2_E = 1.4426950408889634
MASK_VALUE = -1e30

LANES = 128
MXU_COLS = 256
ONES_ROWS = 16
ROW_SPLIT = 4
ATTN_TILE = 256
KEY_TILE = 2 * ATTN_TILE
QUERY_TILE = KEY_TILE
VMEM_LIMIT = 56 * 1024 * 1024


def _tile(n, target, align, col0=0):
    t = (min(target, n) // align) * align
    while n % t or col0 % t:
        t -= align
    return t


def _params(n_grid, flags=None):
    return pltpu.CompilerParams(
        dimension_semantics=("arbitrary",) * n_grid, vmem_limit_bytes=VMEM_LIMIT, flags=flags)


INTERLEAVE_FLAGS = None


def _rmsnorm_kernel(x_ref, g_ref, o_ref, *, eps):
    x = x_ref[...]
    ms = jnp.mean(x * x, axis=-1, keepdims=True)
    o_ref[...] = ((x * lax.rsqrt(ms + eps)) * g_ref[...]).astype(o_ref.dtype)


def _rmsnorm(x, g, *, eps=NORM_EPS, tm=512):
    m, d = x.shape
    tm = min(tm, m)
    return pl.pallas_call(
        functools.partial(_rmsnorm_kernel, eps=eps),
        grid=(m // tm,),
        in_specs=[pl.BlockSpec((tm, d), lambda i: (i, 0)),
                  pl.BlockSpec((1, d), lambda i: (0, 0))],
        out_specs=pl.BlockSpec((tm, d), lambda i: (i, 0)),
        out_shape=jax.ShapeDtypeStruct((m, d), bf16),
        compiler_params=_params(1),
        name="rmsnorm",
    )(x, g.reshape(1, d))


def _rope_tables_kernel(pos_ref, inv_ref, c_ref, sa_ref, sb_ref):
    ang = pos_ref[...].astype(f32) * inv_ref[...]
    lane = lax.broadcasted_iota(jnp.int32, ang.shape, 1)
    c = jnp.cos(ang)
    s = jnp.sin(ang)
    half = ROT_DIM // 2
    c_ref[...] = jnp.where(lane < ROT_DIM, c, 1.0)
    sa_ref[...] = jnp.where((lane >= half) & (lane < ROT_DIM), s, 0.0)
    sb_ref[...] = jnp.where(lane < half, -s, 0.0)


def _rope_tables(positions, *, tm=1024):
    s = positions.shape[0]
    tm = min(tm, s)
    inv = ROPE_THETA ** (-jnp.arange(0, ROT_DIM, 2, dtype=f32) / ROT_DIM)
    inv_lane = jnp.concatenate(
        [inv, inv, jnp.zeros((HEAD_DIM - ROT_DIM,), f32)]).reshape(1, HEAD_DIM)
    spec = pl.BlockSpec((tm, HEAD_DIM), lambda i: (i, 0))
    shape = jax.ShapeDtypeStruct((s, HEAD_DIM), f32)
    return pl.pallas_call(
        _rope_tables_kernel,
        grid=(s // tm,),
        in_specs=[pl.BlockSpec((tm, 1), lambda i: (i, 0)),
                  pl.BlockSpec((1, HEAD_DIM), lambda i: (0, 0))],
        out_specs=[spec, spec, spec],
        out_shape=[shape, shape, shape],
        compiler_params=_params(1),
        name="rope_tables",
    )(positions.reshape(s, 1), inv_lane)


def _in_proj_kernel(a_ref, w_ref, gq_ref, gk_ref, gm_ref, c_ref, sa_ref, sb_ref, o_ref,
                    *, q_tiles, qk_tiles, v_tiles, mem_head):
    j = pl.program_id(1)
    half = ROT_DIM // 2
    rows = a_ref.shape[0] // ROW_SPLIT

    def tile(head, gain, rope):
        chunk = MXU_COLS if head is None else max(head, MXU_COLS)
        for c0 in range(0, o_ref.shape[1], chunk):
            w = w_ref[:, c0:c0 + chunk].astype(bf16)
            for r0 in range(0, a_ref.shape[0], rows):
                acc = jnp.dot(a_ref[r0:r0 + rows, :], w, preferred_element_type=f32)
                if head is None:
                    o_ref[r0:r0 + rows, c0:c0 + chunk] = acc.astype(o_ref.dtype)
                    continue
                for h0 in range(0, chunk, head):
                    t = acc[:, h0:h0 + head]
                    ms = jnp.mean(t * t, axis=-1, keepdims=True)
                    y = (t * lax.rsqrt(ms + NORM_EPS)) * gain
                    if rope:
                        y = (y * c_ref[r0:r0 + rows, :]
                             + pltpu.roll(y, half, axis=1) * sa_ref[r0:r0 + rows, :]
                             + pltpu.roll(y, head - half, axis=1) * sb_ref[r0:r0 + rows, :])
                    o_ref[r0:r0 + rows, c0 + h0:c0 + h0 + head] = y.astype(o_ref.dtype)

    @pl.when(j < qk_tiles)
    def _():
        tile(HEAD_DIM, jnp.where(j < q_tiles, gq_ref[...], gk_ref[...]), True)

    @pl.when((j >= qk_tiles) & (j < qk_tiles + v_tiles))
    def _():
        tile(None, None, False)

    @pl.when(j >= qk_tiles + v_tiles)
    def _():
        tile(mem_head, gm_ref[...], False)


def _in_proj(a, w, layer, self_width, mem_width, g_q, g_k, g_m, tables, *, tm=1024, tn=1024):
    m, k = a.shape
    n = w.shape[2]
    mem_head = mem_width // MEM_HEADS
    tm = min(tm, m)
    tn = _tile(n, tn, mem_head, math.gcd(self_width, mem_width))
    c, sa, sb = tables
    tab_spec = pl.BlockSpec((tm, HEAD_DIM), lambda i, j: (i, 0))
    g_spec = pl.BlockSpec((1, HEAD_DIM), lambda i, j: (0, 0))
    return pl.pallas_call(
        functools.partial(_in_proj_kernel, q_tiles=self_width // tn, qk_tiles=2 * self_width // tn,
                          v_tiles=self_width // tn, mem_head=mem_head),
        grid=(m // tm, n // tn),
        in_specs=[pl.BlockSpec((tm, k), lambda i, j: (i, 0), pipeline_mode=pl.Buffered(1)),
                  pl.BlockSpec((None, k, tn), lambda i, j: (layer, 0, j)),
                  g_spec, g_spec, pl.BlockSpec((1, mem_head), lambda i, j: (0, 0)),
                  tab_spec, tab_spec, tab_spec],
        out_specs=pl.BlockSpec((tm, tn), lambda i, j: (i, j)),
        out_shape=jax.ShapeDtypeStruct((m, n), bf16),
        compiler_params=_params(2),
        name="in_proj",
    )(a, w, g_q.reshape(1, HEAD_DIM), g_k.reshape(1, HEAD_DIM), g_m.reshape(1, mem_head), c, sa, sb)


def _matmul_kernel(a_ref, w_ref, o_ref):
    o_ref[...] = jnp.dot(a_ref[...], w_ref[...].astype(bf16),
                         preferred_element_type=f32).astype(o_ref.dtype)


def _matmul(a, w, col0, n, out_dtype, *, tm=1024, tn=1024):
    m, k = a.shape
    tm = min(tm, m)
    tn = _tile(n, tn, LANES, col0)
    off = col0 // tn
    return pl.pallas_call(
        _matmul_kernel,
        grid=(m // tm, n // tn),
        in_specs=[pl.BlockSpec((tm, k), lambda i, j: (i, 0)),
                  pl.BlockSpec((k, tn), lambda i, j: (0, j + off))],
        out_specs=pl.BlockSpec((tm, tn), lambda i, j: (i, j)),
        out_shape=jax.ShapeDtypeStruct((m, n), out_dtype),
        compiler_params=_params(2),
        name="matmul",
    )(a, w)


def _outproj_kernel(a1_ref, a2_ref, w1_ref, w2_ref, r_ref, o_ref):
    acc = jnp.dot(a1_ref[...], w1_ref[...].astype(bf16), preferred_element_type=f32)
    acc = acc + jnp.dot(a2_ref[...], w2_ref[...].astype(bf16), preferred_element_type=f32)
    o_ref[...] = r_ref[...] + acc


def _outproj(a1, a2, w, layer, resid, *, tm=1024, tn=512):
    m, k1 = a1.shape
    k2 = a2.shape[1]
    n = w.shape[2]
    tm = min(tm, m)
    tn = _tile(n, tn, LANES)
    assert k1 % k2 == 0
    return pl.pallas_call(
        _outproj_kernel,
        grid=(m // tm, n // tn),
        in_specs=[pl.BlockSpec((tm, k1), lambda i, j: (i, 0)),
                  pl.BlockSpec((tm, k2), lambda i, j: (i, 0)),
                  pl.BlockSpec((None, k1, tn), lambda i, j: (layer, 0, j)),
                  pl.BlockSpec((None, k2, tn), lambda i, j: (layer, k1 // k2, j)),
                  pl.BlockSpec((tm, tn), lambda i, j: (i, j))],
        out_specs=pl.BlockSpec((tm, tn), lambda i, j: (i, j)),
        out_shape=jax.ShapeDtypeStruct((m, n), f32),
        compiler_params=_params(2),
        name="outproj",
    )(a1, a2, w, w, resid)


def _ffn_up_kernel(a_ref, wg_ref, wu_ref, o_ref):
    wg = wg_ref[...].astype(bf16)
    wu = wu_ref[...].astype(bf16)
    rows = a_ref.shape[0] // ROW_SPLIT
    for r0 in range(0, a_ref.shape[0], rows):
        a = a_ref[r0:r0 + rows, :]
        g = jnp.dot(a, wg, preferred_element_type=f32)
        u = jnp.dot(a, wu, preferred_element_type=f32)
        o_ref[r0:r0 + rows, :] = ((g * jax.nn.sigmoid(g)) * u).astype(o_ref.dtype)


def _ffn_up(a, wg, wu, layer, *, tm=2048, tn=256):
    m, k = a.shape
    n = wg.shape[2]
    tm = min(tm, m)
    tn = _tile(n, tn, LANES)
    w_spec = pl.BlockSpec((None, k, tn), lambda i, j: (layer, 0, j))
    return pl.pallas_call(
        _ffn_up_kernel,
        grid=(m // tm, n // tn),
        in_specs=[pl.BlockSpec((tm, k), lambda i, j: (i, 0)), w_spec, w_spec],
        out_specs=pl.BlockSpec((tm, tn), lambda i, j: (i, j)),
        out_shape=jax.ShapeDtypeStruct((m, n), bf16),
        compiler_params=_params(2),
        name="ffn_up",
    )(a, wg, wu)


def _ffn_down_kernel(a_ref, w_ref, r_ref, o_ref):
    o_ref[...] = r_ref[...] + jnp.dot(a_ref[...], w_ref[...].astype(bf16),
                                      preferred_element_type=f32)


def _ffn_down(a, w, layer, resid, *, tm=1024, tn=256):
    m, k = a.shape
    n = w.shape[2]
    tm = min(tm, m)
    tn = _tile(n, tn, LANES)
    return pl.pallas_call(
        _ffn_down_kernel,
        grid=(m // tm, n // tn),
        in_specs=[pl.BlockSpec((tm, k), lambda i, j: (i, 0), pipeline_mode=pl.Buffered(1)),
                  pl.BlockSpec((None, k, tn), lambda i, j: (layer, 0, j)),
                  pl.BlockSpec((tm, tn), lambda i, j: (i, j))],
        out_specs=pl.BlockSpec((tm, tn), lambda i, j: (i, j)),
        out_shape=jax.ShapeDtypeStruct((m, n), f32),
        compiler_params=_params(2),
        name="ffn_down",
    )(a, w, resid)


def _transpose_bf16(x):
    return x.T


def _flash_streams(streams, i, t_scr, acc_scr, *, scale):
    t, tk = QUERY_TILE, KEY_TILE
    assert t == tk
    c = scale * LOG2_E
    n_streams = len(streams)
    n_past = i
    key = lax.broadcasted_iota(jnp.int32, (tk, t), 0)
    qry = lax.broadcasted_iota(jnp.int32, (tk, t), 1)
    causal = key <= qry

    def stage_scores(n, slot, diagonal):
        raws = [jnp.dot(load_k(n, diagonal), q_rhs, preferred_element_type=f32)
                for q_rhs, load_k, _, _ in streams]
        maxima = []
        for idx, raw in enumerate(raws):
            s = raw * c
            if diagonal:
                s = jnp.where(causal, s, MASK_VALUE)
            t_scr[slot, idx] = s
            maxima.append(jnp.max(s, axis=0, keepdims=True))
        return maxima

    def accumulate(m_old, maxima, slot, v_pair):
        m_out = []
        for idx, (_, _, load_v_t, _) in enumerate(streams):
            m_new = jnp.maximum(m_old[idx], maxima[idx])
            alpha = jnp.exp2(m_old[idx] - m_new)
            p = jnp.exp2(t_scr[slot, idx] - m_new)
            pv = jnp.dot(load_v_t(v_pair), p.astype(bf16), preferred_element_type=f32)
            acc_scr[idx] = alpha * acc_scr[idx] + pv
            m_out.append(m_new)
        return m_out

    maxima = stage_scores(n_past, 0, True)
    m_run = [jnp.full((1, t), -jnp.inf, f32) for _ in streams]
    for idx in range(n_streams):
        acc_scr[idx] = jnp.zeros(acc_scr.shape[1:], f32)

    def step(n, slot, m_run, maxima):
        maxima_next = stage_scores(n, 1 - slot, False)
        m_run = accumulate(m_run, maxima, slot, jnp.where(n == 0, n_past, n - 1))
        return m_run, maxima_next

    def body(trip, carry):
        m_run, maxima = list(carry[:n_streams]), list(carry[n_streams:])
        m_run, maxima = step(2 * trip, 0, m_run, maxima)
        m_run, maxima = step(2 * trip + 1, 1, m_run, maxima)
        return tuple(m_run + maxima)

    carry = lax.fori_loop(0, n_past // 2, body, tuple(m_run + maxima))
    m_run, maxima = list(carry[:n_streams]), list(carry[n_streams:])

    @pl.when(n_past % 2 == 1)
    def _():
        m_mid, maxima_last = step(n_past - 1, 0, m_run, maxima)
        accumulate(m_mid, maxima_last, 1, n_past - 1)

    @pl.when(n_past % 2 == 0)
    def _():
        accumulate(m_run, maxima, 0, jnp.maximum(n_past - 1, 0))

    return [(acc_scr[idx, :dv, :], acc_scr[idx, dv:dv + 1, :])
            for idx, (_, _, _, dv) in enumerate(streams)]


def _fill_v_t(v_ref, v_t_ref, dv):
    t = KEY_TILE
    rows = dv + ONES_ROWS
    for n in range(v_t_ref.shape[0]):
        v_t = _transpose_bf16(v_ref[n * t:(n + 1) * t, :])
        for h in range(v_ref.shape[1] // dv):
            v_t_ref[n, h * rows:h * rows + dv, :] = v_t[h * dv:(h + 1) * dv]
            v_t_ref[n, h * rows + dv:(h + 1) * rows, :] = jnp.ones((ONES_ROWS, t), bf16)


def _moba_kernel(q_ref, k_ref, v_ref, o_ref, v_t_ref, kmean_ref, t_scr, acc_scr, *, n_sel):
    i = pl.program_id(1)
    t, tk, blk_len = QUERY_TILE, KEY_TILE, MOBA_BLOCK
    d = HEAD_DIM
    nb = kmean_ref.shape[0]
    group = q_ref.shape[1] // d
    v_rows = d + ONES_ROWS

    @pl.when(i == 0)
    def _():
        _fill_v_t(v_ref, v_t_ref, d)
        for n in range(nb):
            kb = k_ref[n * blk_len:(n + 1) * blk_len, :].astype(f32)
            kmean_ref[n:n + 1, :] = jnp.sum(kb, axis=0, keepdims=True) / float(blk_len)

    q_t = _transpose_bf16(q_ref[...])
    blk = lax.broadcasted_iota(jnp.int32, (nb, t), 0)
    own = (tk // blk_len) * i + lax.broadcasted_iota(jnp.int32, (nb, t), 1) // blk_len
    past = blk < own
    lane = lax.broadcasted_iota(jnp.int32, (tk, d), 1)
    sub_block = lax.broadcasted_iota(jnp.int32, (tk, d), 0) // blk_len

    def block_onehot(n):
        return jnp.where(lane == (tk // blk_len) * n + sub_block, 1.0, 0.0).astype(bf16)

    streams = []
    for g in range(group):
        rows = slice(g * d, (g + 1) * d)
        gate = jnp.dot(kmean_ref[:, rows].astype(bf16), q_t[rows], preferred_element_type=f32)
        gt = jnp.where(past, gate, -jnp.inf)
        sel = jnp.zeros(gate.shape, jnp.bool_)
        for _ in range(n_sel):
            mx = jnp.max(gt, axis=0, keepdims=True)
            idx = jnp.min(jnp.where(gt == mx, blk, nb), axis=0, keepdims=True)
            pick = blk == idx
            sel = sel | pick
            gt = jnp.where(pick, -jnp.inf, gt)
        mask_rows = jnp.concatenate(
            [jnp.where((sel & past) | (blk == own), 0.0, MASK_VALUE), jnp.zeros((d - nb, t), f32)],
            axis=0).astype(bf16)
        streams.append((
            jnp.concatenate([q_t[rows], mask_rows], axis=0),
            lambda n, diagonal, rows=rows: jnp.concatenate(
                [k_ref[pl.ds(pl.multiple_of(n * tk, tk), tk), rows], block_onehot(n)], axis=1),
            lambda n, g=g: v_t_ref[n, g * v_rows:(g + 1) * v_rows, :],
            d))

    outs = _flash_streams(streams, i, t_scr, acc_scr, scale=d ** -0.5)
    for g, (acc, l) in enumerate(outs):
        o_ref[:, g * d:(g + 1) * d] = (acc * (1.0 / l)).astype(o_ref.dtype).T


def _moba_attention(proj, n_heads, *, group=4):
    s = proj.shape[0]
    t = QUERY_TILE
    assert t % MOBA_BLOCK == 0 and s % t == 0 and n_heads % group == 0
    nb = s // MOBA_BLOCK
    assert nb <= HEAD_DIM
    v_rows = HEAD_DIM + ONES_ROWS
    n_sel = min(MOBA_TOPK, max(nb - 1, 1))
    w = group * HEAD_DIM
    n_groups = n_heads // group
    return pl.pallas_call(
        functools.partial(_moba_kernel, n_sel=n_sel),
        grid=(n_groups, s // t),
        in_specs=[pl.BlockSpec((t, w), lambda h, i: (i, h)),
                  pl.BlockSpec((s, w), lambda h, i: (0, n_groups + h)),
                  pl.BlockSpec((s, w), lambda h, i: (0, 2 * n_groups + h))],
        out_specs=pl.BlockSpec((t, w), lambda h, i: (i, h)),
        out_shape=jax.ShapeDtypeStruct((s, n_heads * HEAD_DIM), bf16),
        scratch_shapes=[pltpu.VMEM((s // KEY_TILE, group * v_rows, KEY_TILE), bf16),
                        pltpu.VMEM((nb, w), f32),
                        pltpu.VMEM((2, group, KEY_TILE, t), f32),
                        pltpu.VMEM((group, v_rows, t), f32)],
        compiler_params=_params(2, INTERLEAVE_FLAGS),
        name="moba_attention",
    )(proj, proj, proj)


def _diff_kernel(q_ref, k_ref, v_ref, lq1_ref, lk1_ref, lq2_ref, lk2_ref, g_ref, o_ref, v_t_ref,
                 t_scr, acc_scr, *, lam_init):
    i = pl.program_id(1)
    tk = KEY_TILE
    d = HEAD_DIM
    group = q_ref.shape[1] // (2 * d)

    v_rows = 2 * d + ONES_ROWS

    @pl.when(i == 0)
    def _():
        _fill_v_t(v_ref, v_t_ref, 2 * d)

    lam = (jnp.exp(jnp.sum(lq1_ref[...] * lk1_ref[...], axis=-1, keepdims=True))
           - jnp.exp(jnp.sum(lq2_ref[...] * lk2_ref[...], axis=-1, keepdims=True))
           + lam_init)
    q_t = _transpose_bf16(q_ref[...])
    streams = []
    for a in range(group):
        vrows = slice(a * v_rows, (a + 1) * v_rows)
        for sub in range(2):
            rows = slice((2 * a + sub) * d, (2 * a + sub + 1) * d)
            streams.append((
                q_t[rows],
                lambda n, diagonal, rows=rows: k_ref[pl.ds(pl.multiple_of(n * tk, tk), tk), rows],
                lambda n, vrows=vrows: v_t_ref[n, vrows, :],
                2 * d))
    outs = _flash_streams(streams, i, t_scr, acc_scr, scale=d ** -0.5)
    for a in range(group):
        (acc1, l1), (acc2, l2) = outs[2 * a], outs[2 * a + 1]
        o = acc1 * (1.0 / l1) - lam * (acc2 * (1.0 / l2))
        ms = jnp.mean(o * o, axis=0, keepdims=True)
        y = (o * lax.rsqrt(ms + SUBLN_EPS)) * g_ref[...]
        o_ref[:, a * 2 * d:(a + 1) * 2 * d] = (y * (1.0 - lam_init)).astype(o_ref.dtype).T


def _diff_attention(proj, n_heads, lq1, lk1, lq2, lk2, g_subln, lam_init, *, group=2):
    s = proj.shape[0]
    t = QUERY_TILE
    assert s % t == 0 and n_heads % group == 0
    d2 = 2 * HEAD_DIM
    w = group * d2
    n_groups = n_heads // group
    vec = pl.BlockSpec((1, HEAD_DIM), lambda h, i: (0, 0))
    return pl.pallas_call(
        functools.partial(_diff_kernel, lam_init=lam_init),
        grid=(n_groups, s // t),
        in_specs=[pl.BlockSpec((t, w), lambda h, i: (i, h)),
                  pl.BlockSpec((s, w), lambda h, i: (0, n_groups + h)),
                  pl.BlockSpec((s, w), lambda h, i: (0, 2 * n_groups + h)),
                  vec, vec, vec, vec,
                  pl.BlockSpec((d2, 1), lambda h, i: (0, 0))],
        out_specs=pl.BlockSpec((t, w), lambda h, i: (i, h)),
        out_shape=jax.ShapeDtypeStruct((s, n_heads * d2), bf16),
        scratch_shapes=[pltpu.VMEM((s // KEY_TILE, group * (d2 + ONES_ROWS), KEY_TILE), bf16),
                        pltpu.VMEM((2, 2 * group, KEY_TILE, t), f32),
                        pltpu.VMEM((2 * group, d2 + ONES_ROWS, t), f32)],
        compiler_params=_params(2, INTERLEAVE_FLAGS),
        name="diff_attention",
    )(proj, proj, proj, lq1.reshape(1, -1), lk1.reshape(1, -1), lq2.reshape(1, -1), lk2.reshape(1, -1),
      g_subln.reshape(d2, 1))


def _mem_attn_kernel(q_ref, mk_ref, mv_ref, g_ref, o_ref, *, n_heads):
    dm = q_ref.shape[1] // n_heads
    scale = dm ** -0.5
    for h in range(n_heads):
        cols = slice(h * dm, (h + 1) * dm)
        kraw = mk_ref[:, cols]
        ms = jnp.mean(kraw * kraw, axis=-1, keepdims=True)
        kn = ((kraw * lax.rsqrt(ms + NORM_EPS)) * g_ref[...]).astype(bf16)
        s = lax.dot_general(q_ref[:, cols], kn, (((1,), (1,)), ((), ())),
                            preferred_element_type=f32) * scale
        m = jnp.max(s, axis=-1, keepdims=True)
        p = jnp.exp(s - m)
        l = jnp.sum(p, axis=-1, keepdims=True)
        o = jnp.dot(p.astype(bf16), mv_ref[:, cols].astype(bf16), preferred_element_type=f32)
        o_ref[:, cols] = (o * (1.0 / l)).astype(o_ref.dtype)


def _mem_attention(proj, mkv, g_knorm, *, tq=512):
    s = proj.shape[0]
    mlen, wm = mkv.shape[0], mkv.shape[1] // 2
    q_block = proj.shape[1] // wm - 1
    tq = min(tq, s)
    dm = wm // MEM_HEADS
    return pl.pallas_call(
        functools.partial(_mem_attn_kernel, n_heads=MEM_HEADS),
        grid=(s // tq,),
        in_specs=[pl.BlockSpec((tq, wm), lambda i: (i, q_block)),
                  pl.BlockSpec((mlen, wm), lambda i: (0, 0)),
                  pl.BlockSpec((mlen, wm), lambda i: (0, 1)),
                  pl.BlockSpec((1, dm), lambda i: (0, 0))],
        out_specs=pl.BlockSpec((tq, wm), lambda i: (i, 0)),
        out_shape=jax.ShapeDtypeStruct((s, wm), bf16),
        compiler_params=_params(1),
        name="mem_attention",
    )(proj, mkv, mkv, g_knorm.reshape(1, dm))


def kernel(x, mem, positions, g_attn_norm, w_in, w_out, g_qnorm, g_knorm, g_mem_qnorm, g_mem_knorm, g_mem_norm, w_mem_kv, lambda_q1, lambda_k1, lambda_q2, lambda_k2, g_subln, g_ffn_norm, w_gate, w_up, w_down):
    b, s, d_model = x.shape
    assert b == 1
    depth = w_in.shape[0]
    mem_width = w_mem_kv.shape[1] // 2
    self_width = w_out.shape[1] - mem_width
    n_heads = self_width // HEAD_DIM
    mem_head = mem_width // MEM_HEADS

    xs = x.reshape(s, d_model)
    tables = _rope_tables(positions.reshape(s))
    mem_n = _rmsnorm(mem.reshape(mem.shape[1], d_model), g_mem_norm)
    mkv = _matmul(mem_n, w_mem_kv, 0, 2 * mem_width, f32)

    for i in range(depth):
        h = _rmsnorm(xs, g_attn_norm[i])
        proj = _in_proj(h, w_in, i, self_width, mem_width, g_qnorm[i], g_knorm[i], g_mem_qnorm[i],
                        tables)
        if i % 2 == 0:
            self_out = _moba_attention(proj, n_heads)
        else:
            j = i // 2
            lam_init = 0.8 - 0.6 * math.exp(-0.3 * i)
            self_out = _diff_attention(proj, n_heads // 2, lambda_q1[j], lambda_k1[j],
                                       lambda_q2[j], lambda_k2[j], g_subln[j], lam_init)
        mem_out = _mem_attention(proj, mkv, g_mem_knorm[i])
        xs = _outproj(self_out, mem_out, w_out, i, xs)
        f = _rmsnorm(xs, g_ffn_norm[i])
        act = _ffn_up(f, w_gate, w_up, i)
        xs = _ffn_down(act, w_down, i, xs)
    return xs.reshape(b, s, d_model)
```

```python
import functools
import math

import jax
import jax.numpy as jnp
from jax import lax
from jax.experimental import pallas as pl
from jax.experimental.pallas import tpu as pltpu

f32 = jnp.float32
bf16 = jnp.bfloat16

HEAD_DIM = 128
MEM_HEADS = 4
ROT_DIM = HEAD_DIM // 4
ROPE_THETA = 500000.0
MOBA_BLOCK = 256
MOBA_TOPK = 3
NORM_EPS = 1e-6
SUBLN_EPS = 1e-5
LOG2_E = 1.4426950408889634
MASK_VALUE = -1e30

LANES = 128
MXU_COLS = 256
ONES_ROWS = 16
ROW_SPLIT = 4
ATTN_TILE = 256
KEY_TILE = 2 * ATTN_TILE
QUERY_TILE = KEY_TILE
VMEM_LIMIT = 56 * 1024 * 1024


def _tile(n, target, align, col0=0):
    t = (min(target, n) // align) * align
    while n % t or col0 % t:
        t -= align
    return t


def _params(n_grid, flags=None):
    return pltpu.CompilerParams(
        dimension_semantics=("arbitrary",) * n_grid, vmem_limit_bytes=VMEM_LIMIT, flags=flags)


INTERLEAVE_FLAGS = None


def _rmsnorm_kernel(x_ref, g_ref, o_ref, *, eps):
    x = x_ref[...]
    ms = jnp.mean(x * x, axis=-1, keepdims=True)
    o_ref[...] = ((x * lax.rsqrt(ms + eps)) * g_ref[...]).astype(o_ref.dtype)


def _rmsnorm(x, g, *, eps=NORM_EPS, tm=512):
    m, d = x.shape
    tm = min(tm, m)
    return pl.pallas_call(
        functools.partial(_rmsnorm_kernel, eps=eps),
        grid=(m // tm,),
        in_specs=[pl.BlockSpec((tm, d), lambda i: (i, 0)),
                  pl.BlockSpec((1, d), lambda i: (0, 0))],
        out_specs=pl.BlockSpec((tm, d), lambda i: (i, 0)),
        out_shape=jax.ShapeDtypeStruct((m, d), bf16),
        compiler_params=_params(1),
        name="rmsnorm",
    )(x, g.reshape(1, d))


def _rope_tables_kernel(pos_ref, inv_ref, c_ref, sa_ref, sb_ref):
    ang = pos_ref[...].astype(f32) * inv_ref[...]
    lane = lax.broadcasted_iota(jnp.int32, ang.shape, 1)
    c = jnp.cos(ang)
    s = jnp.sin(ang)
    half = ROT_DIM // 2
    c_ref[...] = jnp.where(lane < ROT_DIM, c, 1.0)
    sa_ref[...] = jnp.where((lane >= half) & (lane < ROT_DIM), s, 0.0)
    sb_ref[...] = jnp.where(lane < half, -s, 0.0)


def _rope_tables(positions, *, tm=1024):
    s = positions.shape[0]
    tm = min(tm, s)
    inv = ROPE_THETA ** (-jnp.arange(0, ROT_DIM, 2, dtype=f32) / ROT_DIM)
    inv_lane = jnp.concatenate(
        [inv, inv, jnp.zeros((HEAD_DIM - ROT_DIM,), f32)]).reshape(1, HEAD_DIM)
    spec = pl.BlockSpec((tm, HEAD_DIM), lambda i: (i, 0))
    shape = jax.ShapeDtypeStruct((s, HEAD_DIM), f32)
    return pl.pallas_call(
        _rope_tables_kernel,
        grid=(s // tm,),
        in_specs=[pl.BlockSpec((tm, 1), lambda i: (i, 0)),
                  pl.BlockSpec((1, HEAD_DIM), lambda i: (0, 0))],
        out_specs=[spec, spec, spec],
        out_shape=[shape, shape, shape],
        compiler_params=_params(1),
        name="rope_tables",
    )(positions.reshape(s, 1), inv_lane)


def _in_proj_kernel(a_ref, w_ref, gq_ref, gk_ref, gm_ref, c_ref, sa_ref, sb_ref, o_ref,
                    *, q_tiles, qk_tiles, v_tiles, mem_head):
    j = pl.program_id(1)
    half = ROT_DIM // 2
    rows = a_ref.shape[0] // ROW_SPLIT

    def tile(head, gain, rope):
        chunk = MXU_COLS if head is None else max(head, MXU_COLS)
        for c0 in range(0, o_ref.shape[1], chunk):
            w = w_ref[:, c0:c0 + chunk].astype(bf16)
            for r0 in range(0, a_ref.shape[0], rows):
                acc = jnp.dot(a_ref[r0:r0 + rows, :], w, preferred_element_type=f32)
                if head is None:
                    o_ref[r0:r0 + rows, c0:c0 + chunk] = acc.astype(o_ref.dtype)
                    continue
                for h0 in range(0, chunk, head):
                    t = acc[:, h0:h0 + head]
                    ms = jnp.mean(t * t, axis=-1, keepdims=True)
                    y = (t * lax.rsqrt(ms + NORM_EPS)) * gain
                    if rope:
                        y = (y * c_ref[r0:r0 + rows, :]
                             + pltpu.roll(y, half, axis=1) * sa_ref[r0:r0 + rows, :]
                             + pltpu.roll(y, head - half, axis=1) * sb_ref[r0:r0 + rows, :])
                    o_ref[r0:r0 + rows, c0 + h0:c0 + h0 + head] = y.astype(o_ref.dtype)

    @pl.when(j < qk_tiles)
    def _():
        tile(HEAD_DIM, jnp.where(j < q_tiles, gq_ref[...], gk_ref[...]), True)

    @pl.when((j >= qk_tiles) & (j < qk_tiles + v_tiles))
    def _():
        tile(None, None, False)

    @pl.when(j >= qk_tiles + v_tiles)
    def _():
        tile(mem_head, gm_ref[...], False)


def _in_proj(a, w, layer, self_width, mem_width, g_q, g_k, g_m, tables, *, tm=1024, tn=1024):
    m, k = a.shape
    n = w.shape[2]
    mem_head = mem_width // MEM_HEADS
    tm = min(tm, m)
    tn = _tile(n, tn, mem_head, math.gcd(self_width, mem_width))
    c, sa, sb = tables
    tab_spec = pl.BlockSpec((tm, HEAD_DIM), lambda i, j: (i, 0))
    g_spec = pl.BlockSpec((1, HEAD_DIM), lambda i, j: (0, 0))
    return pl.pallas_call(
        functools.partial(_in_proj_kernel, q_tiles=self_width // tn, qk_tiles=2 * self_width // tn,
                          v_tiles=self_width // tn, mem_head=mem_head),
        grid=(m // tm, n // tn),
        in_specs=[pl.BlockSpec((tm, k), lambda i, j: (i, 0), pipeline_mode=pl.Buffered(1)),
                  pl.BlockSpec((None, k, tn), lambda i, j: (layer, 0, j)),
                  g_spec, g_spec, pl.BlockSpec((1, mem_head), lambda i, j: (0, 0)),
                  tab_spec, tab_spec, tab_spec],
        out_specs=pl.BlockSpec((tm, tn), lambda i, j: (i, j)),
        out_shape=jax.ShapeDtypeStruct((m, n), bf16),
        compiler_params=_params(2),
        name="in_proj",
    )(a, w, g_q.reshape(1, HEAD_DIM), g_k.reshape(1, HEAD_DIM), g_m.reshape(1, mem_head), c, sa, sb)


def _matmul_kernel(a_ref, w_ref, o_ref):
    o_ref[...] = jnp.dot(a_ref[...], w_ref[...].astype(bf16),
                         preferred_element_type=f32).astype(o_ref.dtype)


def _matmul(a, w, col0, n, out_dtype, *, tm=1024, tn=1024):
    m, k = a.shape
    tm = min(tm, m)
    tn = _tile(n, tn, LANES, col0)
    off = col0 // tn
    return pl.pallas_call(
        _matmul_kernel,
        grid=(m // tm, n // tn),
        in_specs=[pl.BlockSpec((tm, k), lambda i, j: (i, 0)),
                  pl.BlockSpec((k, tn), lambda i, j: (0, j + off))],
        out_specs=pl.BlockSpec((tm, tn), lambda i, j: (i, j)),
        out_shape=jax.ShapeDtypeStruct((m, n), out_dtype),
        compiler_params=_params(2),
        name="matmul",
    )(a, w)


def _outproj_kernel(a1_ref, a2_ref, w1_ref, w2_ref, r_ref, o_ref):
    w1 = w1_ref[...].astype(bf16)
    w2 = w2_ref[...].astype(bf16)
    rows = o_ref.shape[0] // ROW_SPLIT
    for r0 in range(0, o_ref.shape[0], rows):
        rs = slice(r0, r0 + rows)
        acc = jnp.dot(a1_ref[rs, :], w1, preferred_element_type=f32)
        acc = acc + jnp.dot(a2_ref[rs, :], w2, preferred_element_type=f32)
        o_ref[rs, :] = r_ref[rs, :] + acc


def _outproj(a1, a2, w, layer, resid, *, tm=1024, tn=512):
    m, k1 = a1.shape
    k2 = a2.shape[1]
    n = w.shape[2]
    tm = min(tm, m)
    tn = _tile(n, tn, LANES)
    assert k1 % k2 == 0
    return pl.pallas_call(
        _outproj_kernel,
        grid=(m // tm, n // tn),
        in_specs=[pl.BlockSpec((tm, k1), lambda i, j: (i, 0)),
                  pl.BlockSpec((tm, k2), lambda i, j: (i, 0)),
                  pl.BlockSpec((None, k1, tn), lambda i, j: (layer, 0, j)),
                  pl.BlockSpec((None, k2, tn), lambda i, j: (layer, k1 // k2, j)),
                  pl.BlockSpec((tm, tn), lambda i, j: (i, j))],
        out_specs=pl.BlockSpec((tm, tn), lambda i, j: (i, j)),
        out_shape=jax.ShapeDtypeStruct((m, n), f32),
        compiler_params=_params(2),
        name="outproj",
    )(a1, a2, w, w, resid)


def _ffn_up_kernel(a_ref, wg_ref, wu_ref, o_ref):
    wg = wg_ref[...].astype(bf16)
    wu = wu_ref[...].astype(bf16)
    rows = a_ref.shape[0] // ROW_SPLIT
    for r0 in range(0, a_ref.shape[0], rows):
        a = a_ref[r0:r0 + rows, :]
        g = jnp.dot(a, wg, preferred_element_type=f32)
        u = jnp.dot(a, wu, preferred_element_type=f32)
        o_ref[r0:r0 + rows, :] = ((g * jax.nn.sigmoid(g)) * u).astype(o_ref.dtype)


def _ffn_up(a, wg, wu, layer, *, tm=2048, tn=256):
    m, k = a.shape
    n = wg.shape[2]
    tm = min(tm, m)
    tn = _tile(n, tn, LANES)
    w_spec = pl.BlockSpec((None, k, tn), lambda i, j: (layer, 0, j))
    return pl.pallas_call(
        _ffn_up_kernel,
        grid=(m // tm, n // tn),
        in_specs=[pl.BlockSpec((tm, k), lambda i, j: (i, 0)), w_spec, w_spec],
        out_specs=pl.BlockSpec((tm, tn), lambda i, j: (i, j)),
        out_shape=jax.ShapeDtypeStruct((m, n), bf16),
        compiler_params=_params(2),
        name="ffn_up",
    )(a, wg, wu)


def _ffn_down_kernel(a_ref, w_ref, r_ref, o_ref):
    w = w_ref[...].astype(bf16)
    rows = o_ref.shape[0] // ROW_SPLIT
    for r0 in range(0, o_ref.shape[0], rows):
        rs = slice(r0, r0 + rows)
        o_ref[rs, :] = r_ref[rs, :] + jnp.dot(a_ref[rs, :], w, preferred_element_type=f32)


def _ffn_down(a, w, layer, resid, *, tm=1024, tn=256):
    m, k = a.shape
    n = w.shape[2]
    tm = min(tm, m)
    tn = _tile(n, tn, LANES)
    return pl.pallas_call(
        _ffn_down_kernel,
        grid=(m // tm, n // tn),
        in_specs=[pl.BlockSpec((tm, k), lambda i, j: (i, 0), pipeline_mode=pl.Buffered(1)),
                  pl.BlockSpec((None, k, tn), lambda i, j: (layer, 0, j)),
                  pl.BlockSpec((tm, tn), lambda i, j: (i, j))],
        out_specs=pl.BlockSpec((tm, tn), lambda i, j: (i, j)),
        out_shape=jax.ShapeDtypeStruct((m, n), f32),
        compiler_params=_params(2),
        name="ffn_down",
    )(a, w, resid)


def _transpose_bf16(x):
    return x.T


def _flash_streams(streams, i, t_scr, acc_scr, *, scale):
    t, tk = QUERY_TILE, KEY_TILE
    assert t == tk
    c = scale * LOG2_E
    n_streams = len(streams)
    n_past = i
    key = lax.broadcasted_iota(jnp.int32, (tk, t), 0)
    qry = lax.broadcasted_iota(jnp.int32, (tk, t), 1)
    causal = key <= qry

    def stage_scores(n, slot, diagonal):
        raws = [jnp.dot(load_k(n, diagonal), q_rhs, preferred_element_type=f32)
                for q_rhs, load_k, _, _ in streams]
        maxima = []
        for idx, raw in enumerate(raws):
            s = raw * c
            if diagonal:
                s = jnp.where(causal, s, MASK_VALUE)
            t_scr[slot, idx] = s
            maxima.append(jnp.max(s, axis=0, keepdims=True))
        return maxima

    def accumulate(m_old, maxima, slot, v_pair):
        m_out = []
        for idx, (_, _, load_v_t, _) in enumerate(streams):
            m_new = jnp.maximum(m_old[idx], maxima[idx])
            alpha = jnp.exp2(m_old[idx] - m_new)
            p = jnp.exp2(t_scr[slot, idx] - m_new)
            pv = jnp.dot(load_v_t(v_pair), p.astype(bf16), preferred_element_type=f32)
            acc_scr[idx] = alpha * acc_scr[idx] + pv
            m_out.append(m_new)
        return m_out

    maxima = stage_scores(n_past, 0, True)
    m_run = [jnp.full((1, t), -jnp.inf, f32) for _ in streams]
    for idx in range(n_streams):
        acc_scr[idx] = jnp.zeros(acc_scr.shape[1:], f32)

    def step(n, slot, m_run, maxima):
        maxima_next = stage_scores(n, 1 - slot, False)
        m_run = accumulate(m_run, maxima, slot, jnp.where(n == 0, n_past, n - 1))
        return m_run, maxima_next

    def body(trip, carry):
        m_run, maxima = list(carry[:n_streams]), list(carry[n_streams:])
        m_run, maxima = step(2 * trip, 0, m_run, maxima)
        m_run, maxima = step(2 * trip + 1, 1, m_run, maxima)
        return tuple(m_run + maxima)

    carry = lax.fori_loop(0, n_past // 2, body, tuple(m_run + maxima))
    m_run, maxima = list(carry[:n_streams]), list(carry[n_streams:])

    @pl.when(n_past % 2 == 1)
    def _():
        m_mid, maxima_last = step(n_past - 1, 0, m_run, maxima)
        accumulate(m_mid, maxima_last, 1, n_past - 1)

    @pl.when(n_past % 2 == 0)
    def _():
        accumulate(m_run, maxima, 0, jnp.maximum(n_past - 1, 0))

    return [(acc_scr[idx, :dv, :], acc_scr[idx, dv:dv + 1, :])
            for idx, (_, _, _, dv) in enumerate(streams)]


def _fill_v_t(v_ref, v_t_ref, dv):
    t = KEY_TILE
    rows = dv + ONES_ROWS
    for n in range(v_t_ref.shape[0]):
        v_t = _transpose_bf16(v_ref[n * t:(n + 1) * t, :])
        for h in range(v_ref.shape[1] // dv):
            v_t_ref[n, h * rows:h * rows + dv, :] = v_t[h * dv:(h + 1) * dv]
            v_t_ref[n, h * rows + dv:(h + 1) * rows, :] = jnp.ones((ONES_ROWS, t), bf16)


def _moba_kernel(q_ref, k_ref, v_ref, o_ref, v_t_ref, kmean_ref, t_scr, acc_scr, *, n_sel):
    i = pl.program_id(1)
    t, tk, blk_len = QUERY_TILE, KEY_TILE, MOBA_BLOCK
    d = HEAD_DIM
    nb = kmean_ref.shape[0]
    group = q_ref.shape[1] // d
    v_rows = d + ONES_ROWS

    @pl.when(i == 0)
    def _():
        _fill_v_t(v_ref, v_t_ref, d)
        for n in range(nb):
            kb = k_ref[n * blk_len:(n + 1) * blk_len, :].astype(f32)
            kmean_ref[n:n + 1, :] = jnp.sum(kb, axis=0, keepdims=True) / float(blk_len)

    q_t = _transpose_bf16(q_ref[...])
    blk = lax.broadcasted_iota(jnp.int32, (nb, t), 0)
    own = (tk // blk_len) * i + lax.broadcasted_iota(jnp.int32, (nb, t), 1) // blk_len
    past = blk < own
    lane = lax.broadcasted_iota(jnp.int32, (tk, d), 1)
    sub_block = lax.broadcasted_iota(jnp.int32, (tk, d), 0) // blk_len

    def block_onehot(n):
        return jnp.where(lane == (tk // blk_len) * n + sub_block, 1.0, 0.0).astype(bf16)

    streams = []
    for g in range(group):
        rows = slice(g * d, (g + 1) * d)
        gate = jnp.dot(kmean_ref[:, rows].astype(bf16), q_t[rows], preferred_element_type=f32)
        gt = jnp.where(past, gate, -jnp.inf)
        sel = jnp.zeros(gate.shape, jnp.bool_)
        for _ in range(n_sel):
            mx = jnp.max(gt, axis=0, keepdims=True)
            idx = jnp.min(jnp.where(gt == mx, blk, nb), axis=0, keepdims=True)
            pick = blk == idx
            sel = sel | pick
            gt = jnp.where(pick, -jnp.inf, gt)
        mask_rows = jnp.concatenate(
            [jnp.where((sel & past) | (blk == own), 0.0, MASK_VALUE), jnp.zeros((d - nb, t), f32)],
            axis=0).astype(bf16)
        streams.append((
            jnp.concatenate([q_t[rows], mask_rows], axis=0),
            lambda n, diagonal, rows=rows: jnp.concatenate(
                [k_ref[pl.ds(pl.multiple_of(n * tk, tk), tk), rows], block_onehot(n)], axis=1),
            lambda n, g=g: v_t_ref[n, g * v_rows:(g + 1) * v_rows, :],
            d))

    outs = _flash_streams(streams, i, t_scr, acc_scr, scale=d ** -0.5)
    for g, (acc, l) in enumerate(outs):
        o_ref[:, g * d:(g + 1) * d] = (acc * (1.0 / l)).astype(o_ref.dtype).T


def _moba_attention(proj, n_heads, *, group=4):
    s = proj.shape[0]
    t = QUERY_TILE
    assert t % MOBA_BLOCK == 0 and s % t == 0 and n_heads % group == 0
    nb = s // MOBA_BLOCK
    assert nb <= HEAD_DIM
    v_rows = HEAD_DIM + ONES_ROWS
    n_sel = min(MOBA_TOPK, max(nb - 1, 1))
    w = group * HEAD_DIM
    n_groups = n_heads // group
    return pl.pallas_call(
        functools.partial(_moba_kernel, n_sel=n_sel),
        grid=(n_groups, s // t),
        in_specs=[pl.BlockSpec((t, w), lambda h, i: (i, h)),
                  pl.BlockSpec((s, w), lambda h, i: (0, n_groups + h)),
                  pl.BlockSpec((s, w), lambda h, i: (0, 2 * n_groups + h))],
        out_specs=pl.BlockSpec((t, w), lambda h, i: (i, h)),
        out_shape=jax.ShapeDtypeStruct((s, n_heads * HEAD_DIM), bf16),
        scratch_shapes=[pltpu.VMEM((s // KEY_TILE, group * v_rows, KEY_TILE), bf16),
                        pltpu.VMEM((nb, w), f32),
                        pltpu.VMEM((2, group, KEY_TILE, t), f32),
                        pltpu.VMEM((group, v_rows, t), f32)],
        compiler_params=_params(2, INTERLEAVE_FLAGS),
        name="moba_attention",
    )(proj, proj, proj)


def _diff_kernel(q_ref, k_ref, v_ref, lq1_ref, lk1_ref, lq2_ref, lk2_ref, g_ref, o_ref, v_t_ref,
                 t_scr, acc_scr, *, lam_init):
    i = pl.program_id(1)
    tk = KEY_TILE
    d = HEAD_DIM
    group = q_ref.shape[1] // (2 * d)

    v_rows = 2 * d + ONES_ROWS

    @pl.when(i == 0)
    def _():
        _fill_v_t(v_ref, v_t_ref, 2 * d)

    lam = (jnp.exp(jnp.sum(lq1_ref[...] * lk1_ref[...], axis=-1, keepdims=True))
           - jnp.exp(jnp.sum(lq2_ref[...] * lk2_ref[...], axis=-1, keepdims=True))
           + lam_init)
    q_t = _transpose_bf16(q_ref[...])
    streams = []
    for a in range(group):
        vrows = slice(a * v_rows, (a + 1) * v_rows)
        for sub in range(2):
            rows = slice((2 * a + sub) * d, (2 * a + sub + 1) * d)
            streams.append((
                q_t[rows],
                lambda n, diagonal, rows=rows: k_ref[pl.ds(pl.multiple_of(n * tk, tk), tk), rows],
                lambda n, vrows=vrows: v_t_ref[n, vrows, :],
                2 * d))
    outs = _flash_streams(streams, i, t_scr, acc_scr, scale=d ** -0.5)
    for a in range(group):
        (acc1, l1), (acc2, l2) = outs[2 * a], outs[2 * a + 1]
        o = acc1 * (1.0 / l1) - lam * (acc2 * (1.0 / l2))
        ms = jnp.mean(o * o, axis=0, keepdims=True)
        y = (o * lax.rsqrt(ms + SUBLN_EPS)) * g_ref[...]
        o_ref[:, a * 2 * d:(a + 1) * 2 * d] = (y * (1.0 - lam_init)).astype(o_ref.dtype).T


def _diff_attention(proj, n_heads, lq1, lk1, lq2, lk2, g_subln, lam_init, *, group=2):
    s = proj.shape[0]
    t = QUERY_TILE
    assert s % t == 0 and n_heads % group == 0
    d2 = 2 * HEAD_DIM
    w = group * d2
    n_groups = n_heads // group
    vec = pl.BlockSpec((1, HEAD_DIM), lambda h, i: (0, 0))
    return pl.pallas_call(
        functools.partial(_diff_kernel, lam_init=lam_init),
        grid=(n_groups, s // t),
        in_specs=[pl.BlockSpec((t, w), lambda h, i: (i, h)),
                  pl.BlockSpec((s, w), lambda h, i: (0, n_groups + h)),
                  pl.BlockSpec((s, w), lambda h, i: (0, 2 * n_groups + h)),
                  vec, vec, vec, vec,
                  pl.BlockSpec((d2, 1), lambda h, i: (0, 0))],
        out_specs=pl.BlockSpec((t, w), lambda h, i: (i, h)),
        out_shape=jax.ShapeDtypeStruct((s, n_heads * d2), bf16),
        scratch_shapes=[pltpu.VMEM((s // KEY_TILE, group * (d2 + ONES_ROWS), KEY_TILE), bf16),
                        pltpu.VMEM((2, 2 * group, KEY_TILE, t), f32),
                        pltpu.VMEM((2 * group, d2 + ONES_ROWS, t), f32)],
        compiler_params=_params(2, INTERLEAVE_FLAGS),
        name="diff_attention",
    )(proj, proj, proj, lq1.reshape(1, -1), lk1.reshape(1, -1), lq2.reshape(1, -1), lk2.reshape(1, -1),
      g_subln.reshape(d2, 1))


def _mem_attn_kernel(q_ref, mk_ref, mv_ref, g_ref, o_ref, *, n_heads):
    dm = q_ref.shape[1] // n_heads
    scale = dm ** -0.5
    for h in range(n_heads):
        cols = slice(h * dm, (h + 1) * dm)
        kraw = mk_ref[:, cols]
        ms = jnp.mean(kraw * kraw, axis=-1, keepdims=True)
        kn = ((kraw * lax.rsqrt(ms + NORM_EPS)) * g_ref[...]).astype(bf16)
        s = lax.dot_general(q_ref[:, cols], kn, (((1,), (1,)), ((), ())),
                            preferred_element_type=f32) * scale
        m = jnp.max(s, axis=-1, keepdims=True)
        p = jnp.exp(s - m)
        l = jnp.sum(p, axis=-1, keepdims=True)
        o = jnp.dot(p.astype(bf16), mv_ref[:, cols].astype(bf16), preferred_element_type=f32)
        o_ref[:, cols] = (o * (1.0 / l)).astype(o_ref.dtype)


def _mem_attention(proj, mkv, g_knorm, *, tq=512):
    s = proj.shape[0]
    mlen, wm = mkv.shape[0], mkv.shape[1] // 2
    q_block = proj.shape[1] // wm - 1
    tq = min(tq, s)
    dm = wm // MEM_HEADS
    return pl.pallas_call(
        functools.partial(_mem_attn_kernel, n_heads=MEM_HEADS),
        grid=(s // tq,),
        in_specs=[pl.BlockSpec((tq, wm), lambda i: (i, q_block)),
                  pl.BlockSpec((mlen, wm), lambda i: (0, 0)),
                  pl.BlockSpec((mlen, wm), lambda i: (0, 1)),
                  pl.BlockSpec((1, dm), lambda i: (0, 0))],
        out_specs=pl.BlockSpec((tq, wm), lambda i: (i, 0)),
        out_shape=jax.ShapeDtypeStruct((s, wm), bf16),
        compiler_params=_params(1),
        name="mem_attention",
    )(proj, mkv, mkv, g_knorm.reshape(1, dm))


def kernel(x, mem, positions, g_attn_norm, w_in, w_out, g_qnorm, g_knorm, g_mem_qnorm, g_mem_knorm, g_mem_norm, w_mem_kv, lambda_q1, lambda_k1, lambda_q2, lambda_k2, g_subln, g_ffn_norm, w_gate, w_up, w_down):
    b, s, d_model = x.shape
    assert b == 1
    depth = w_in.shape[0]
    mem_width = w_mem_kv.shape[1] // 2
    self_width = w_out.shape[1] - mem_width
    n_heads = self_width // HEAD_DIM
    mem_head = mem_width // MEM_HEADS

    xs = x.reshape(s, d_model)
    tables = _rope_tables(positions.reshape(s))
    mem_n = _rmsnorm(mem.reshape(mem.shape[1], d_model), g_mem_norm)
    mkv = _matmul(mem_n, w_mem_kv, 0, 2 * mem_width, f32)

    for i in range(depth):
        h = _rmsnorm(xs, g_attn_norm[i])
        proj = _in_proj(h, w_in, i, self_width, mem_width, g_qnorm[i], g_knorm[i], g_mem_qnorm[i],
                        tables)
        if i % 2 == 0:
            self_out = _moba_attention(proj, n_heads)
        else:
            j = i // 2
            lam_init = 0.8 - 0.6 * math.exp(-0.3 * i)
            self_out = _diff_attention(proj, n_heads // 2, lambda_q1[j], lambda_k1[j],
                                       lambda_q2[j], lambda_k2[j], g_subln[j], lam_init)
        mem_out = _mem_attention(proj, mkv, g_mem_knorm[i])
        xs = _outproj(self_out, mem_out, w_out, i, xs)
        f = _rmsnorm(xs, g_ffn_norm[i])
        act = _ffn_up(f, w_gate, w_up, i)
        xs = _ffn_down(act, w_down, i, xs)
    return xs.reshape(b, s, d_model)
```

```python
import functools
import math

import jax
import jax.numpy as jnp
from jax import lax
from jax.experimental import pallas as pl
from jax.experimental.pallas import tpu as pltpu

f32 = jnp.float32
bf16 = jnp.bfloat16

HEAD_DIM = 128
MEM_HEADS = 4
ROT_DIM = HEAD_DIM // 4
ROPE_THETA = 500000.0
MOBA_BLOCK = 256
MOBA_TOPK = 3
NORM_EPS = 1e-6
SUBLN_EPS = 1e-5
LOG2_E = 1.4426950408889634
MASK_VALUE = -1e30

LANES = 128
MXU_COLS = 256
ONES_ROWS = 16
ROW_SPLIT = 4
ATTN_TILE = 256
KEY_TILE = 2 * ATTN_TILE
QUERY_TILE = KEY_TILE
VMEM_LIMIT = 56 * 1024 * 1024


def _tile(n, target, align, col0=0):
    t = (min(target, n) // align) * align
    while n % t or col0 % t:
        t -= align
    return t


def _params(n_grid, flags=None):
    return pltpu.CompilerParams(
        dimension_semantics=("arbitrary",) * n_grid, vmem_limit_bytes=VMEM_LIMIT, flags=flags)


INTERLEAVE_FLAGS = None


def _rmsnorm_kernel(x_ref, g_ref, o_ref, *, eps):
    x = x_ref[...]
    ms = jnp.mean(x * x, axis=-1, keepdims=True)
    o_ref[...] = ((x * lax.rsqrt(ms + eps)) * g_ref[...]).astype(o_ref.dtype)


def _rmsnorm(x, g, *, eps=NORM_EPS, tm=512):
    m, d = x.shape
    tm = min(tm, m)
    return pl.pallas_call(
        functools.partial(_rmsnorm_kernel, eps=eps),
        grid=(m // tm,),
        in_specs=[pl.BlockSpec((tm, d), lambda i: (i, 0)),
                  pl.BlockSpec((1, d), lambda i: (0, 0))],
        out_specs=pl.BlockSpec((tm, d), lambda i: (i, 0)),
        out_shape=jax.ShapeDtypeStruct((m, d), bf16),
        compiler_params=_params(1),
        name="rmsnorm",
    )(x, g.reshape(1, d))


def _rope_tables_kernel(pos_ref, inv_ref, c_ref, sa_ref, sb_ref):
    ang = pos_ref[...].astype(f32) * inv_ref[...]
    lane = lax.broadcasted_iota(jnp.int32, ang.shape, 1)
    c = jnp.cos(ang)
    s = jnp.sin(ang)
    half = ROT_DIM // 2
    c_ref[...] = jnp.where(lane < ROT_DIM, c, 1.0)
    sa_ref[...] = jnp.where((lane >= half) & (lane < ROT_DIM), s, 0.0)
    sb_ref[...] = jnp.where(lane < half, -s, 0.0)


def _rope_tables(positions, *, tm=1024):
    s = positions.shape[0]
    tm = min(tm, s)
    inv = ROPE_THETA ** (-jnp.arange(0, ROT_DIM, 2, dtype=f32) / ROT_DIM)
    inv_lane = jnp.concatenate(
        [inv, inv, jnp.zeros((HEAD_DIM - ROT_DIM,), f32)]).reshape(1, HEAD_DIM)
    spec = pl.BlockSpec((tm, HEAD_DIM), lambda i: (i, 0))
    shape = jax.ShapeDtypeStruct((s, HEAD_DIM), f32)
    return pl.pallas_call(
        _rope_tables_kernel,
        grid=(s // tm,),
        in_specs=[pl.BlockSpec((tm, 1), lambda i: (i, 0)),
                  pl.BlockSpec((1, HEAD_DIM), lambda i: (0, 0))],
        out_specs=[spec, spec, spec],
        out_shape=[shape, shape, shape],
        compiler_params=_params(1),
        name="rope_tables",
    )(positions.reshape(s, 1), inv_lane)


def _in_proj_kernel(a_ref, w_ref, gq_ref, gk_ref, gm_ref, c_ref, sa_ref, sb_ref, o_ref,
                    *, q_tiles, qk_tiles, v_tiles, mem_head):
    j = pl.program_id(1)
    half = ROT_DIM // 2
    rows = a_ref.shape[0] // ROW_SPLIT

    def tile(head, gain, rope):
        chunk = MXU_COLS if head is None else max(head, MXU_COLS)
        for c0 in range(0, o_ref.shape[1], chunk):
            w = w_ref[:, c0:c0 + chunk].astype(bf16)
            for r0 in range(0, a_ref.shape[0], rows):
                acc = jnp.dot(a_ref[r0:r0 + rows, :], w, preferred_element_type=f32)
                if head is None:
                    o_ref[r0:r0 + rows, c0:c0 + chunk] = acc.astype(o_ref.dtype)
                    continue
                for h0 in range(0, chunk, head):
                    t = acc[:, h0:h0 + head]
                    ms = jnp.mean(t * t, axis=-1, keepdims=True)
                    y = (t * lax.rsqrt(ms + NORM_EPS)) * gain
                    if rope:
                        y = (y * c_ref[r0:r0 + rows, :]
                             + pltpu.roll(y, half, axis=1) * sa_ref[r0:r0 + rows, :]
                             + pltpu.roll(y, head - half, axis=1) * sb_ref[r0:r0 + rows, :])
                    o_ref[r0:r0 + rows, c0 + h0:c0 + h0 + head] = y.astype(o_ref.dtype)

    @pl.when(j < qk_tiles)
    def _():
        tile(HEAD_DIM, jnp.where(j < q_tiles, gq_ref[...], gk_ref[...]), True)

    @pl.when((j >= qk_tiles) & (j < qk_tiles + v_tiles))
    def _():
        tile(None, None, False)

    @pl.when(j >= qk_tiles + v_tiles)
    def _():
        tile(mem_head, gm_ref[...], False)


def _in_proj(a, w, layer, self_width, mem_width, g_q, g_k, g_m, tables, *, tm=1024, tn=1024):
    m, k = a.shape
    n = w.shape[2]
    mem_head = mem_width // MEM_HEADS
    tm = min(tm, m)
    tn = _tile(n, tn, mem_head, math.gcd(self_width, mem_width))
    c, sa, sb = tables
    tab_spec = pl.BlockSpec((tm, HEAD_DIM), lambda i, j: (i, 0))
    g_spec = pl.BlockSpec((1, HEAD_DIM), lambda i, j: (0, 0))
    return pl.pallas_call(
        functools.partial(_in_proj_kernel, q_tiles=self_width // tn, qk_tiles=2 * self_width // tn,
                          v_tiles=self_width // tn, mem_head=mem_head),
        grid=(m // tm, n // tn),
        in_specs=[pl.BlockSpec((tm, k), lambda i, j: (i, 0), pipeline_mode=pl.Buffered(1)),
                  pl.BlockSpec((None, k, tn), lambda i, j: (layer, 0, j)),
                  g_spec, g_spec, pl.BlockSpec((1, mem_head), lambda i, j: (0, 0)),
                  tab_spec, tab_spec, tab_spec],
        out_specs=pl.BlockSpec((tm, tn), lambda i, j: (i, j)),
        out_shape=jax.ShapeDtypeStruct((m, n), bf16),
        compiler_params=_params(2),
        name="in_proj",
    )(a, w, g_q.reshape(1, HEAD_DIM), g_k.reshape(1, HEAD_DIM), g_m.reshape(1, mem_head), c, sa, sb)


def _matmul_kernel(a_ref, w_ref, o_ref):
    o_ref[...] = jnp.dot(a_ref[...], w_ref[...].astype(bf16),
                         preferred_element_type=f32).astype(o_ref.dtype)


def _matmul(a, w, col0, n, out_dtype, *, tm=1024, tn=1024):
    m, k = a.shape
    tm = min(tm, m)
    tn = _tile(n, tn, LANES, col0)
    off = col0 // tn
    return pl.pallas_call(
        _matmul_kernel,
        grid=(m // tm, n // tn),
        in_specs=[pl.BlockSpec((tm, k), lambda i, j: (i, 0)),
                  pl.BlockSpec((k, tn), lambda i, j: (0, j + off))],
        out_specs=pl.BlockSpec((tm, tn), lambda i, j: (i, j)),
        out_shape=jax.ShapeDtypeStruct((m, n), out_dtype),
        compiler_params=_params(2),
        name="matmul",
    )(a, w)


def _outproj_kernel(a1_ref, a2_ref, w1_ref, w2_ref, r_ref, o_ref):
    acc = jnp.dot(a1_ref[...], w1_ref[...].astype(bf16), preferred_element_type=f32)
    acc = acc + jnp.dot(a2_ref[...], w2_ref[...].astype(bf16), preferred_element_type=f32)
    o_ref[...] = r_ref[...] + acc


def _outproj(a1, a2, w, layer, resid, *, tm=2048, tn=256):
    m, k1 = a1.shape
    k2 = a2.shape[1]
    n = w.shape[2]
    tm = min(tm, m)
    tn = _tile(n, tn, LANES)
    assert k1 % k2 == 0
    return pl.pallas_call(
        _outproj_kernel,
        grid=(m // tm, n // tn),
        in_specs=[pl.BlockSpec((tm, k1), lambda i, j: (i, 0), pipeline_mode=pl.Buffered(1)),
                  pl.BlockSpec((tm, k2), lambda i, j: (i, 0), pipeline_mode=pl.Buffered(1)),
                  pl.BlockSpec((None, k1, tn), lambda i, j: (layer, 0, j)),
                  pl.BlockSpec((None, k2, tn), lambda i, j: (layer, k1 // k2, j)),
                  pl.BlockSpec((tm, tn), lambda i, j: (i, j))],
        out_specs=pl.BlockSpec((tm, tn), lambda i, j: (i, j)),
        out_shape=jax.ShapeDtypeStruct((m, n), f32),
        compiler_params=_params(2),
        name="outproj",
    )(a1, a2, w, w, resid)


def _ffn_up_kernel(a_ref, wg_ref, wu_ref, o_ref):
    wg = wg_ref[...].astype(bf16)
    wu = wu_ref[...].astype(bf16)
    rows = a_ref.shape[0] // (2 * ROW_SPLIT)
    for r0 in range(0, a_ref.shape[0], rows):
        a = a_ref[r0:r0 + rows, :]
        g = jnp.dot(a, wg, preferred_element_type=f32)
        u = jnp.dot(a, wu, preferred_element_type=f32)
        o_ref[r0:r0 + rows, :] = ((g * jax.nn.sigmoid(g)) * u).astype(o_ref.dtype)


def _ffn_up(a, wg, wu, layer, *, tm=2048, tn=256):
    m, k = a.shape
    n = wg.shape[2]
    tm = min(tm, m)
    tn = _tile(n, tn, LANES)
    w_spec = pl.BlockSpec((None, k, tn), lambda i, j: (layer, 0, j))
    return pl.pallas_call(
        _ffn_up_kernel,
        grid=(m // tm, n // tn),
        in_specs=[pl.BlockSpec((tm, k), lambda i, j: (i, 0)), w_spec, w_spec],
        out_specs=pl.BlockSpec((tm, tn), lambda i, j: (i, j)),
        out_shape=jax.ShapeDtypeStruct((m, n), bf16),
        compiler_params=_params(2),
        name="ffn_up",
    )(a, wg, wu)


def _ffn_down_kernel(a_ref, w_ref, r_ref, o_ref):
    o_ref[...] = r_ref[...] + jnp.dot(a_ref[...], w_ref[...].astype(bf16),
                                      preferred_element_type=f32)


def _ffn_down(a, w, layer, resid, *, tm=1024, tn=256):
    m, k = a.shape
    n = w.shape[2]
    tm = min(tm, m)
    tn = _tile(n, tn, LANES)
    return pl.pallas_call(
        _ffn_down_kernel,
        grid=(m // tm, n // tn),
        in_specs=[pl.BlockSpec((tm, k), lambda i, j: (i, 0), pipeline_mode=pl.Buffered(1)),
                  pl.BlockSpec((None, k, tn), lambda i, j: (layer, 0, j)),
                  pl.BlockSpec((tm, tn), lambda i, j: (i, j))],
        out_specs=pl.BlockSpec((tm, tn), lambda i, j: (i, j)),
        out_shape=jax.ShapeDtypeStruct((m, n), f32),
        compiler_params=_params(2),
        name="ffn_down",
    )(a, w, resid)


def _transpose_bf16(x):
    return x.T


def _flash_streams(streams, i, t_scr, acc_scr, *, scale):
    t, tk = QUERY_TILE, KEY_TILE
    assert t == tk
    c = scale * LOG2_E
    n_streams = len(streams)
    n_past = i
    key = lax.broadcasted_iota(jnp.int32, (tk, t), 0)
    qry = lax.broadcasted_iota(jnp.int32, (tk, t), 1)
    causal = key <= qry

    def stage_scores(n, slot, diagonal):
        raws = [jnp.dot(load_k(n, diagonal), q_rhs, preferred_element_type=f32)
                for q_rhs, load_k, _, _ in streams]
        maxima = []
        for idx, raw in enumerate(raws):
            s = raw * c
            if diagonal:
                s = jnp.where(causal, s, MASK_VALUE)
            t_scr[slot, idx] = s
            maxima.append(jnp.max(s, axis=0, keepdims=True))
        return maxima

    def accumulate(m_old, maxima, slot, v_pair):
        m_out = []
        for idx, (_, _, load_v_t, _) in enumerate(streams):
            m_new = jnp.maximum(m_old[idx], maxima[idx])
            alpha = jnp.exp2(m_old[idx] - m_new)
            p = jnp.exp2(t_scr[slot, idx] - m_new)
            pv = jnp.dot(load_v_t(v_pair), p.astype(bf16), preferred_element_type=f32)
            acc_scr[idx] = alpha * acc_scr[idx] + pv
            m_out.append(m_new)
        return m_out

    maxima = stage_scores(n_past, 0, True)
    m_run = [jnp.full((1, t), -jnp.inf, f32) for _ in streams]
    for idx in range(n_streams):
        acc_scr[idx] = jnp.zeros(acc_scr.shape[1:], f32)

    def step(n, slot, m_run, maxima):
        maxima_next = stage_scores(n, 1 - slot, False)
        m_run = accumulate(m_run, maxima, slot, jnp.where(n == 0, n_past, n - 1))
        return m_run, maxima_next

    def body(trip, carry):
        m_run, maxima = list(carry[:n_streams]), list(carry[n_streams:])
        m_run, maxima = step(2 * trip, 0, m_run, maxima)
        m_run, maxima = step(2 * trip + 1, 1, m_run, maxima)
        return tuple(m_run + maxima)

    carry = lax.fori_loop(0, n_past // 2, body, tuple(m_run + maxima))
    m_run, maxima = list(carry[:n_streams]), list(carry[n_streams:])

    @pl.when(n_past % 2 == 1)
    def _():
        m_mid, maxima_last = step(n_past - 1, 0, m_run, maxima)
        accumulate(m_mid, maxima_last, 1, n_past - 1)

    @pl.when(n_past % 2 == 0)
    def _():
        accumulate(m_run, maxima, 0, jnp.maximum(n_past - 1, 0))

    return [(acc_scr[idx, :dv, :], acc_scr[idx, dv:dv + 1, :])
            for idx, (_, _, _, dv) in enumerate(streams)]


def _fill_v_t(v_ref, v_t_ref, dv):
    t = KEY_TILE
    rows = dv + ONES_ROWS
    for n in range(v_t_ref.shape[0]):
        v_t = _transpose_bf16(v_ref[n * t:(n + 1) * t, :])
        for h in range(v_ref.shape[1] // dv):
            v_t_ref[n, h * rows:h * rows + dv, :] = v_t[h * dv:(h + 1) * dv]
            v_t_ref[n, h * rows + dv:(h + 1) * rows, :] = jnp.ones((ONES_ROWS, t), bf16)


def _moba_kernel(q_ref, k_ref, v_ref, o_ref, v_t_ref, kmean_ref, t_scr, acc_scr, *, n_sel):
    i = pl.program_id(1)
    t, tk, blk_len = QUERY_TILE, KEY_TILE, MOBA_BLOCK
    d = HEAD_DIM
    nb = kmean_ref.shape[0]
    group = q_ref.shape[1] // d
    v_rows = d + ONES_ROWS

    @pl.when(i == 0)
    def _():
        _fill_v_t(v_ref, v_t_ref, d)
        for n in range(nb):
            kb = k_ref[n * blk_len:(n + 1) * blk_len, :].astype(f32)
            kmean_ref[n:n + 1, :] = jnp.sum(kb, axis=0, keepdims=True) / float(blk_len)

    q_t = _transpose_bf16(q_ref[...])
    blk = lax.broadcasted_iota(jnp.int32, (nb, t), 0)
    own = (tk // blk_len) * i + lax.broadcasted_iota(jnp.int32, (nb, t), 1) // blk_len
    past = blk < own
    lane = lax.broadcasted_iota(jnp.int32, (tk, d), 1)
    sub_block = lax.broadcasted_iota(jnp.int32, (tk, d), 0) // blk_len

    def block_onehot(n):
        return jnp.where(lane == (tk // blk_len) * n + sub_block, 1.0, 0.0).astype(bf16)

    streams = []
    for g in range(group):
        rows = slice(g * d, (g + 1) * d)
        gate = jnp.dot(kmean_ref[:, rows].astype(bf16), q_t[rows], preferred_element_type=f32)
        gt = jnp.where(past, gate, -jnp.inf)
        sel = jnp.zeros(gate.shape, jnp.bool_)
        for _ in range(n_sel):
            mx = jnp.max(gt, axis=0, keepdims=True)
            idx = jnp.min(jnp.where(gt == mx, blk, nb), axis=0, keepdims=True)
            pick = blk == idx
            sel = sel | pick
            gt = jnp.where(pick, -jnp.inf, gt)
        mask_rows = jnp.concatenate(
            [jnp.where((sel & past) | (blk == own), 0.0, MASK_VALUE), jnp.zeros((d - nb, t), f32)],
            axis=0).astype(bf16)
        streams.append((
            jnp.concatenate([q_t[rows], mask_rows], axis=0),
            lambda n, diagonal, rows=rows: jnp.concatenate(
                [k_ref[pl.ds(pl.multiple_of(n * tk, tk), tk), rows], block_onehot(n)], axis=1),
            lambda n, g=g: v_t_ref[n, g * v_rows:(g + 1) * v_rows, :],
            d))

    outs = _flash_streams(streams, i, t_scr, acc_scr, scale=d ** -0.5)
    for g, (acc, l) in enumerate(outs):
        o_ref[:, g * d:(g + 1) * d] = (acc * (1.0 / l)).astype(o_ref.dtype).T


def _moba_attention(proj, n_heads, *, group=4):
    s = proj.shape[0]
    t = QUERY_TILE
    assert t % MOBA_BLOCK == 0 and s % t == 0 and n_heads % group == 0
    nb = s // MOBA_BLOCK
    assert nb <= HEAD_DIM
    v_rows = HEAD_DIM + ONES_ROWS
    n_sel = min(MOBA_TOPK, max(nb - 1, 1))
    w = group * HEAD_DIM
    n_groups = n_heads // group
    return pl.pallas_call(
        functools.partial(_moba_kernel, n_sel=n_sel),
        grid=(n_groups, s // t),
        in_specs=[pl.BlockSpec((t, w), lambda h, i: (i, h)),
                  pl.BlockSpec((s, w), lambda h, i: (0, n_groups + h)),
                  pl.BlockSpec((s, w), lambda h, i: (0, 2 * n_groups + h))],
        out_specs=pl.BlockSpec((t, w), lambda h, i: (i, h)),
        out_shape=jax.ShapeDtypeStruct((s, n_heads * HEAD_DIM), bf16),
        scratch_shapes=[pltpu.VMEM((s // KEY_TILE, group * v_rows, KEY_TILE), bf16),
                        pltpu.VMEM((nb, w), f32),
                        pltpu.VMEM((2, group, KEY_TILE, t), f32),
                        pltpu.VMEM((group, v_rows, t), f32)],
        compiler_params=_params(2, INTERLEAVE_FLAGS),
        name="moba_attention",
    )(proj, proj, proj)


def _diff_kernel(q_ref, k_ref, v_ref, lq1_ref, lk1_ref, lq2_ref, lk2_ref, g_ref, o_ref, v_t_ref,
                 t_scr, acc_scr, *, lam_init):
    i = pl.program_id(1)
    tk = KEY_TILE
    d = HEAD_DIM
    group = q_ref.shape[1] // (2 * d)

    v_rows = 2 * d + ONES_ROWS

    @pl.when(i == 0)
    def _():
        _fill_v_t(v_ref, v_t_ref, 2 * d)

    lam = (jnp.exp(jnp.sum(lq1_ref[...] * lk1_ref[...], axis=-1, keepdims=True))
           - jnp.exp(jnp.sum(lq2_ref[...] * lk2_ref[...], axis=-1, keepdims=True))
           + lam_init)
    q_t = _transpose_bf16(q_ref[...])
    streams = []
    for a in range(group):
        vrows = slice(a * v_rows, (a + 1) * v_rows)
        for sub in range(2):
            rows = slice((2 * a + sub) * d, (2 * a + sub + 1) * d)
            streams.append((
                q_t[rows],
                lambda n, diagonal, rows=rows: k_ref[pl.ds(pl.multiple_of(n * tk, tk), tk), rows],
                lambda n, vrows=vrows: v_t_ref[n, vrows, :],
                2 * d))
    outs = _flash_streams(streams, i, t_scr, acc_scr, scale=d ** -0.5)
    for a in range(group):
        (acc1, l1), (acc2, l2) = outs[2 * a], outs[2 * a + 1]
        o = acc1 * (1.0 / l1) - lam * (acc2 * (1.0 / l2))
        ms = jnp.mean(o * o, axis=0, keepdims=True)
        y = (o * lax.rsqrt(ms + SUBLN_EPS)) * g_ref[...]
        o_ref[:, a * 2 * d:(a + 1) * 2 * d] = (y * (1.0 - lam_init)).astype(o_ref.dtype).T


def _diff_attention(proj, n_heads, lq1, lk1, lq2, lk2, g_subln, lam_init, *, group=2):
    s = proj.shape[0]
    t = QUERY_TILE
    assert s % t == 0 and n_heads % group == 0
    d2 = 2 * HEAD_DIM
    w = group * d2
    n_groups = n_heads // group
    vec = pl.BlockSpec((1, HEAD_DIM), lambda h, i: (0, 0))
    return pl.pallas_call(
        functools.partial(_diff_kernel, lam_init=lam_init),
        grid=(n_groups, s // t),
        in_specs=[pl.BlockSpec((t, w), lambda h, i: (i, h)),
                  pl.BlockSpec((s, w), lambda h, i: (0, n_groups + h)),
                  pl.BlockSpec((s, w), lambda h, i: (0, 2 * n_groups + h)),
                  vec, vec, vec, vec,
                  pl.BlockSpec((d2, 1), lambda h, i: (0, 0))],
        out_specs=pl.BlockSpec((t, w), lambda h, i: (i, h)),
        out_shape=jax.ShapeDtypeStruct((s, n_heads * d2), bf16),
        scratch_shapes=[pltpu.VMEM((s // KEY_TILE, group * (d2 + ONES_ROWS), KEY_TILE), bf16),
                        pltpu.VMEM((2, 2 * group, KEY_TILE, t), f32),
                        pltpu.VMEM((2 * group, d2 + ONES_ROWS, t), f32)],
        compiler_params=_params(2, INTERLEAVE_FLAGS),
        name="diff_attention",
    )(proj, proj, proj, lq1.reshape(1, -1), lk1.reshape(1, -1), lq2.reshape(1, -1), lk2.reshape(1, -1),
      g_subln.reshape(d2, 1))


def _mem_attn_kernel(q_ref, mk_ref, mv_ref, g_ref, o_ref, *, n_heads):
    dm = q_ref.shape[1] // n_heads
    scale = dm ** -0.5
    for h in range(n_heads):
        cols = slice(h * dm, (h + 1) * dm)
        kraw = mk_ref[:, cols]
        ms = jnp.mean(kraw * kraw, axis=-1, keepdims=True)
        kn = ((kraw * lax.rsqrt(ms + NORM_EPS)) * g_ref[...]).astype(bf16)
        s = lax.dot_general(q_ref[:, cols], kn, (((1,), (1,)), ((), ())),
                            preferred_element_type=f32) * scale
        m = jnp.max(s, axis=-1, keepdims=True)
        p = jnp.exp(s - m)
        l = jnp.sum(p, axis=-1, keepdims=True)
        o = jnp.dot(p.astype(bf16), mv_ref[:, cols].astype(bf16), preferred_element_type=f32)
        o_ref[:, cols] = (o * (1.0 / l)).astype(o_ref.dtype)


def _mem_attention(proj, mkv, g_knorm, *, tq=512):
    s = proj.shape[0]
    mlen, wm = mkv.shape[0], mkv.shape[1] // 2
    q_block = proj.shape[1] // wm - 1
    tq = min(tq, s)
    dm = wm // MEM_HEADS
    return pl.pallas_call(
        functools.partial(_mem_attn_kernel, n_heads=MEM_HEADS),
        grid=(s // tq,),
        in_specs=[pl.BlockSpec((tq, wm), lambda i: (i, q_block)),
                  pl.BlockSpec((mlen, wm), lambda i: (0, 0)),
                  pl.BlockSpec((mlen, wm), lambda i: (0, 1)),
                  pl.BlockSpec((1, dm), lambda i: (0, 0))],
        out_specs=pl.BlockSpec((tq, wm), lambda i: (i, 0)),
        out_shape=jax.ShapeDtypeStruct((s, wm), bf16),
        compiler_params=_params(1),
        name="mem_attention",
    )(proj, mkv, mkv, g_knorm.reshape(1, dm))


def kernel(x, mem, positions, g_attn_norm, w_in, w_out, g_qnorm, g_knorm, g_mem_qnorm, g_mem_knorm, g_mem_norm, w_mem_kv, lambda_q1, lambda_k1, lambda_q2, lambda_k2, g_subln, g_ffn_norm, w_gate, w_up, w_down):
    b, s, d_model = x.shape
    assert b == 1
    depth = w_in.shape[0]
    mem_width = w_mem_kv.shape[1] // 2
    self_width = w_out.shape[1] - mem_width
    n_heads = self_width // HEAD_DIM
    mem_head = mem_width // MEM_HEADS

    xs = x.reshape(s, d_model)
    tables = _rope_tables(positions.reshape(s))
    mem_n = _rmsnorm(mem.reshape(mem.shape[1], d_model), g_mem_norm)
    mkv = _matmul(mem_n, w_mem_kv, 0, 2 * mem_width, f32)

    for i in range(depth):
        h = _rmsnorm(xs, g_attn_norm[i])
        proj = _in_proj(h, w_in, i, self_width, mem_width, g_qnorm[i], g_knorm[i], g_mem_qnorm[i],
                        tables)
        if i % 2 == 0:
            self_out = _moba_attention(proj, n_heads)
        else:
            j = i // 2
            lam_init = 0.8 - 0.6 * math.exp(-0.3 * i)
            self_out = _diff_attention(proj, n_heads // 2, lambda_q1[j], lambda_k1[j],
                                       lambda_q2[j], lambda_k2[j], g_subln[j], lam_init)
        mem_out = _mem_attention(proj, mkv, g_mem_knorm[i])
        xs = _outproj(self_out, mem_out, w_out, i, xs)
        f = _rmsnorm(xs, g_ffn_norm[i])
        act = _ffn_up(f, w_gate, w_up, i)
        xs = _ffn_down(act, w_down, i, xs)
    return xs.reshape(b, s, d_model)
```

```python
import functools
import math

import jax
import jax.numpy as jnp
from jax import lax
from jax.experimental import pallas as pl
from jax.experimental.pallas import tpu as pltpu

f32 = jnp.float32
bf16 = jnp.bfloat16

HEAD_DIM = 128
MEM_HEADS = 4
ROT_DIM = HEAD_DIM // 4
ROPE_THETA = 500000.0
MOBA_BLOCK = 256
MOBA_TOPK = 3
NORM_EPS = 1e-6
SUBLN_EPS = 1e-5
LOG2_E = 1.4426950408889634
MASK_VALUE = -1e30

LANES = 128
MXU_COLS = 256
ONES_ROWS = 16
ROW_SPLIT = 4
FFN_UP_SLICE = 512
ATTN_TILE = 256
KEY_TILE = 2 * ATTN_TILE
QUERY_TILE = KEY_TILE
VMEM_LIMIT = 56 * 1024 * 1024


def _tile(n, target, align, col0=0):
    t = (min(target, n) // align) * align
    while n % t or col0 % t:
        t -= align
    return t


def _params(n_grid, flags=None):
    return pltpu.CompilerParams(
        dimension_semantics=("arbitrary",) * n_grid, vmem_limit_bytes=VMEM_LIMIT, flags=flags)


INTERLEAVE_FLAGS = None


def _rmsnorm_kernel(x_ref, g_ref, o_ref, *, eps):
    x = x_ref[...]
    ms = jnp.mean(x * x, axis=-1, keepdims=True)
    o_ref[...] = ((x * lax.rsqrt(ms + eps)) * g_ref[...]).astype(o_ref.dtype)


def _rmsnorm(x, g, *, eps=NORM_EPS, tm=512):
    m, d = x.shape
    tm = min(tm, m)
    return pl.pallas_call(
        functools.partial(_rmsnorm_kernel, eps=eps),
        grid=(m // tm,),
        in_specs=[pl.BlockSpec((tm, d), lambda i: (i, 0)),
                  pl.BlockSpec((1, d), lambda i: (0, 0))],
        out_specs=pl.BlockSpec((tm, d), lambda i: (i, 0)),
        out_shape=jax.ShapeDtypeStruct((m, d), bf16),
        compiler_params=_params(1),
        name="rmsnorm",
    )(x, g.reshape(1, d))


def _rope_tables_kernel(pos_ref, inv_ref, c_ref, sa_ref, sb_ref):
    ang = pos_ref[...].astype(f32) * inv_ref[...]
    lane = lax.broadcasted_iota(jnp.int32, ang.shape, 1)
    c = jnp.cos(ang)
    s = jnp.sin(ang)
    half = ROT_DIM // 2
    c_ref[...] = jnp.where(lane < ROT_DIM, c, 1.0)
    sa_ref[...] = jnp.where((lane >= half) & (lane < ROT_DIM), s, 0.0)
    sb_ref[...] = jnp.where(lane < half, -s, 0.0)


def _rope_tables(positions, *, tm=1024):
    s = positions.shape[0]
    tm = min(tm, s)
    inv = ROPE_THETA ** (-jnp.arange(0, ROT_DIM, 2, dtype=f32) / ROT_DIM)
    inv_lane = jnp.concatenate(
        [inv, inv, jnp.zeros((HEAD_DIM - ROT_DIM,), f32)]).reshape(1, HEAD_DIM)
    spec = pl.BlockSpec((tm, HEAD_DIM), lambda i: (i, 0))
    shape = jax.ShapeDtypeStruct((s, HEAD_DIM), f32)
    return pl.pallas_call(
        _rope_tables_kernel,
        grid=(s // tm,),
        in_specs=[pl.BlockSpec((tm, 1), lambda i: (i, 0)),
                  pl.BlockSpec((1, HEAD_DIM), lambda i: (0, 0))],
        out_specs=[spec, spec, spec],
        out_shape=[shape, shape, shape],
        compiler_params=_params(1),
        name="rope_tables",
    )(positions.reshape(s, 1), inv_lane)


def _in_proj_kernel(a_ref, w_ref, gq_ref, gk_ref, gm_ref, c_ref, sa_ref, sb_ref, o_ref,
                    *, q_tiles, qk_tiles, v_tiles, mem_head):
    j = pl.program_id(1)
    half = ROT_DIM // 2
    rows = a_ref.shape[0] // ROW_SPLIT

    def tile(head, gain, rope):
        chunk = MXU_COLS if head is None else max(head, MXU_COLS)
        for c0 in range(0, o_ref.shape[1], chunk):
            w = w_ref[:, c0:c0 + chunk].astype(bf16)
            for r0 in range(0, a_ref.shape[0], rows):
                acc = jnp.dot(a_ref[r0:r0 + rows, :], w, preferred_element_type=f32)
                if head is None:
                    o_ref[r0:r0 + rows, c0:c0 + chunk] = acc.astype(o_ref.dtype)
                    continue
                for h0 in range(0, chunk, head):
                    t = acc[:, h0:h0 + head]
                    ms = jnp.mean(t * t, axis=-1, keepdims=True)
                    y = (t * lax.rsqrt(ms + NORM_EPS)) * gain
                    if rope:
                        y = (y * c_ref[r0:r0 + rows, :]
                             + pltpu.roll(y, half, axis=1) * sa_ref[r0:r0 + rows, :]
                             + pltpu.roll(y, head - half, axis=1) * sb_ref[r0:r0 + rows, :])
                    o_ref[r0:r0 + rows, c0 + h0:c0 + h0 + head] = y.astype(o_ref.dtype)

    @pl.when(j < qk_tiles)
    def _():
        tile(HEAD_DIM, jnp.where(j < q_tiles, gq_ref[...], gk_ref[...]), True)

    @pl.when((j >= qk_tiles) & (j < qk_tiles + v_tiles))
    def _():
        tile(None, None, False)

    @pl.when(j >= qk_tiles + v_tiles)
    def _():
        tile(mem_head, gm_ref[...], False)


def _in_proj(a, w, layer, self_width, mem_width, g_q, g_k, g_m, tables, *, tm=1024, tn=1024):
    m, k = a.shape
    n = w.shape[2]
    mem_head = mem_width // MEM_HEADS
    tm = min(tm, m)
    tn = _tile(n, tn, mem_head, math.gcd(self_width, mem_width))
    c, sa, sb = tables
    tab_spec = pl.BlockSpec((tm, HEAD_DIM), lambda i, j: (i, 0))
    g_spec = pl.BlockSpec((1, HEAD_DIM), lambda i, j: (0, 0))
    return pl.pallas_call(
        functools.partial(_in_proj_kernel, q_tiles=self_width // tn, qk_tiles=2 * self_width // tn,
                          v_tiles=self_width // tn, mem_head=mem_head),
        grid=(m // tm, n // tn),
        in_specs=[pl.BlockSpec((tm, k), lambda i, j: (i, 0), pipeline_mode=pl.Buffered(1)),
                  pl.BlockSpec((None, k, tn), lambda i, j: (layer, 0, j)),
                  g_spec, g_spec, pl.BlockSpec((1, mem_head), lambda i, j: (0, 0)),
                  tab_spec, tab_spec, tab_spec],
        out_specs=pl.BlockSpec((tm, tn), lambda i, j: (i, j)),
        out_shape=jax.ShapeDtypeStruct((m, n), bf16),
        compiler_params=_params(2),
        name="in_proj",
    )(a, w, g_q.reshape(1, HEAD_DIM), g_k.reshape(1, HEAD_DIM), g_m.reshape(1, mem_head), c, sa, sb)


def _matmul_kernel(a_ref, w_ref, o_ref):
    o_ref[...] = jnp.dot(a_ref[...], w_ref[...].astype(bf16),
                         preferred_element_type=f32).astype(o_ref.dtype)


def _matmul(a, w, col0, n, out_dtype, *, tm=1024, tn=1024):
    m, k = a.shape
    tm = min(tm, m)
    tn = _tile(n, tn, LANES, col0)
    off = col0 // tn
    return pl.pallas_call(
        _matmul_kernel,
        grid=(m // tm, n // tn),
        in_specs=[pl.BlockSpec((tm, k), lambda i, j: (i, 0)),
                  pl.BlockSpec((k, tn), lambda i, j: (0, j + off))],
        out_specs=pl.BlockSpec((tm, tn), lambda i, j: (i, j)),
        out_shape=jax.ShapeDtypeStruct((m, n), out_dtype),
        compiler_params=_params(2),
        name="matmul",
    )(a, w)


def _outproj_kernel(a1_ref, a2_ref, w1_ref, w2_ref, r_ref, o_ref):
    acc = jnp.dot(a1_ref[...], w1_ref[...].astype(bf16), preferred_element_type=f32)
    acc = acc + jnp.dot(a2_ref[...], w2_ref[...].astype(bf16), preferred_element_type=f32)
    o_ref[...] = r_ref[...] + acc


def _outproj(a1, a2, w, layer, resid, *, tm=1024, tn=512):
    m, k1 = a1.shape
    k2 = a2.shape[1]
    n = w.shape[2]
    tm = min(tm, m)
    tn = _tile(n, tn, LANES)
    assert k1 % k2 == 0
    return pl.pallas_call(
        _outproj_kernel,
        grid=(m // tm, n // tn),
        in_specs=[pl.BlockSpec((tm, k1), lambda i, j: (i, 0)),
                  pl.BlockSpec((tm, k2), lambda i, j: (i, 0)),
                  pl.BlockSpec((None, k1, tn), lambda i, j: (layer, 0, j)),
                  pl.BlockSpec((None, k2, tn), lambda i, j: (layer, k1 // k2, j)),
                  pl.BlockSpec((tm, tn), lambda i, j: (i, j))],
        out_specs=pl.BlockSpec((tm, tn), lambda i, j: (i, j)),
        out_shape=jax.ShapeDtypeStruct((m, n), f32),
        compiler_params=_params(2),
        name="outproj",
    )(a1, a2, w, w, resid)


def _ffn_up_kernel(a_ref, wg_ref, wu_ref, o_ref):
    wg = wg_ref[...].astype(bf16)
    wu = wu_ref[...].astype(bf16)
    rows = min(FFN_UP_SLICE, a_ref.shape[0])
    for r0 in range(0, a_ref.shape[0], rows):
        a = a_ref[r0:r0 + rows, :]
        g = jnp.dot(a, wg, preferred_element_type=f32)
        u = jnp.dot(a, wu, preferred_element_type=f32)
        o_ref[r0:r0 + rows, :] = ((g * jax.nn.sigmoid(g)) * u).astype(o_ref.dtype)


def _ffn_up(a, wg, wu, layer, *, tm=4096, tn=256):
    m, k = a.shape
    n = wg.shape[2]
    tm = min(tm, m)
    tn = _tile(n, tn, LANES)
    w_spec = pl.BlockSpec((None, k, tn), lambda i, j: (layer, 0, j))
    return pl.pallas_call(
        _ffn_up_kernel,
        grid=(m // tm, n // tn),
        in_specs=[pl.BlockSpec((tm, k), lambda i, j: (i, 0), pipeline_mode=pl.Buffered(1)),
                  w_spec, w_spec],
        out_specs=pl.BlockSpec((tm, tn), lambda i, j: (i, j)),
        out_shape=jax.ShapeDtypeStruct((m, n), bf16),
        compiler_params=_params(2),
        name="ffn_up",
    )(a, wg, wu)


def _ffn_down_kernel(a_ref, w_ref, r_ref, o_ref):
    o_ref[...] = r_ref[...] + jnp.dot(a_ref[...], w_ref[...].astype(bf16),
                                      preferred_element_type=f32)


def _ffn_down(a, w, layer, resid, *, tm=1024, tn=256):
    m, k = a.shape
    n = w.shape[2]
    tm = min(tm, m)
    tn = _tile(n, tn, LANES)
    return pl.pallas_call(
        _ffn_down_kernel,
        grid=(m // tm, n // tn),
        in_specs=[pl.BlockSpec((tm, k), lambda i, j: (i, 0), pipeline_mode=pl.Buffered(1)),
                  pl.BlockSpec((None, k, tn), lambda i, j: (layer, 0, j)),
                  pl.BlockSpec((tm, tn), lambda i, j: (i, j))],
        out_specs=pl.BlockSpec((tm, tn), lambda i, j: (i, j)),
        out_shape=jax.ShapeDtypeStruct((m, n), f32),
        compiler_params=_params(2),
        name="ffn_down",
    )(a, w, resid)


def _transpose_bf16(x):
    return x.T


def _flash_streams(streams, i, t_scr, acc_scr, *, scale):
    t, tk = QUERY_TILE, KEY_TILE
    assert t == tk
    c = scale * LOG2_E
    n_streams = len(streams)
    n_past = i
    key = lax.broadcasted_iota(jnp.int32, (tk, t), 0)
    qry = lax.broadcasted_iota(jnp.int32, (tk, t), 1)
    causal = key <= qry

    def stage_scores(n, slot, diagonal):
        raws = [jnp.dot(load_k(n, diagonal), q_rhs, preferred_element_type=f32)
                for q_rhs, load_k, _, _ in streams]
        maxima = []
        for idx, raw in enumerate(raws):
            s = raw * c
            if diagonal:
                s = jnp.where(causal, s, MASK_VALUE)
            t_scr[slot, idx] = s
            maxima.append(jnp.max(s, axis=0, keepdims=True))
        return maxima

    def accumulate(m_old, maxima, slot, v_pair):
        m_out = []
        for idx, (_, _, load_v_t, _) in enumerate(streams):
            m_new = jnp.maximum(m_old[idx], maxima[idx])
            alpha = jnp.exp2(m_old[idx] - m_new)
            p = jnp.exp2(t_scr[slot, idx] - m_new)
            pv = jnp.dot(load_v_t(v_pair), p.astype(bf16), preferred_element_type=f32)
            acc_scr[idx] = alpha * acc_scr[idx] + pv
            m_out.append(m_new)
        return m_out

    maxima = stage_scores(n_past, 0, True)
    m_run = [jnp.full((1, t), -jnp.inf, f32) for _ in streams]
    for idx in range(n_streams):
        acc_scr[idx] = jnp.zeros(acc_scr.shape[1:], f32)

    def step(n, slot, m_run, maxima):
        maxima_next = stage_scores(n, 1 - slot, False)
        m_run = accumulate(m_run, maxima, slot, jnp.where(n == 0, n_past, n - 1))
        return m_run, maxima_next

    def body(trip, carry):
        m_run, maxima = list(carry[:n_streams]), list(carry[n_streams:])
        m_run, maxima = step(2 * trip, 0, m_run, maxima)
        m_run, maxima = step(2 * trip + 1, 1, m_run, maxima)
        return tuple(m_run + maxima)

    carry = lax.fori_loop(0, n_past // 2, body, tuple(m_run + maxima))
    m_run, maxima = list(carry[:n_streams]), list(carry[n_streams:])

    @pl.when(n_past % 2 == 1)
    def _():
        m_mid, maxima_last = step(n_past - 1, 0, m_run, maxima)
        accumulate(m_mid, maxima_last, 1, n_past - 1)

    @pl.when(n_past % 2 == 0)
    def _():
        accumulate(m_run, maxima, 0, jnp.maximum(n_past - 1, 0))

    return [(acc_scr[idx, :dv, :], acc_scr[idx, dv:dv + 1, :])
            for idx, (_, _, _, dv) in enumerate(streams)]


def _fill_v_t(v_ref, v_t_ref, dv):
    t = KEY_TILE
    rows = dv + ONES_ROWS
    for n in range(v_t_ref.shape[0]):
        v_t = _transpose_bf16(v_ref[n * t:(n + 1) * t, :])
        for h in range(v_ref.shape[1] // dv):
            v_t_ref[n, h * rows:h * rows + dv, :] = v_t[h * dv:(h + 1) * dv]
            v_t_ref[n, h * rows + dv:(h + 1) * rows, :] = jnp.ones((ONES_ROWS, t), bf16)


def _moba_kernel(q_ref, k_ref, v_ref, o_ref, v_t_ref, kmean_ref, t_scr, acc_scr, *, n_sel):
    i = pl.program_id(1)
    t, tk, blk_len = QUERY_TILE, KEY_TILE, MOBA_BLOCK
    d = HEAD_DIM
    nb = kmean_ref.shape[0]
    group = q_ref.shape[1] // d
    v_rows = d + ONES_ROWS

    @pl.when(i == 0)
    def _():
        _fill_v_t(v_ref, v_t_ref, d)
        for n in range(nb):
            kb = k_ref[n * blk_len:(n + 1) * blk_len, :].astype(f32)
            kmean_ref[n:n + 1, :] = jnp.sum(kb, axis=0, keepdims=True) / float(blk_len)

    q_t = _transpose_bf16(q_ref[...])
    blk = lax.broadcasted_iota(jnp.int32, (nb, t), 0)
    own = (tk // blk_len) * i + lax.broadcasted_iota(jnp.int32, (nb, t), 1) // blk_len
    past = blk < own
    lane = lax.broadcasted_iota(jnp.int32, (tk, d), 1)
    sub_block = lax.broadcasted_iota(jnp.int32, (tk, d), 0) // blk_len

    def block_onehot(n):
        return jnp.where(lane == (tk // blk_len) * n + sub_block, 1.0, 0.0).astype(bf16)

    streams = []
    for g in range(group):
        rows = slice(g * d, (g + 1) * d)
        gate = jnp.dot(kmean_ref[:, rows].astype(bf16), q_t[rows], preferred_element_type=f32)
        gt = jnp.where(past, gate, -jnp.inf)
        sel = jnp.zeros(gate.shape, jnp.bool_)
        for _ in range(n_sel):
            mx = jnp.max(gt, axis=0, keepdims=True)
            idx = jnp.min(jnp.where(gt == mx, blk, nb), axis=0, keepdims=True)
            pick = blk == idx
            sel = sel | pick
            gt = jnp.where(pick, -jnp.inf, gt)
        mask_rows = jnp.concatenate(
            [jnp.where((sel & past) | (blk == own), 0.0, MASK_VALUE), jnp.zeros((d - nb, t), f32)],
            axis=0).astype(bf16)
        streams.append((
            jnp.concatenate([q_t[rows], mask_rows], axis=0),
            lambda n, diagonal, rows=rows: jnp.concatenate(
                [k_ref[pl.ds(pl.multiple_of(n * tk, tk), tk), rows], block_onehot(n)], axis=1),
            lambda n, g=g: v_t_ref[n, g * v_rows:(g + 1) * v_rows, :],
            d))

    outs = _flash_streams(streams, i, t_scr, acc_scr, scale=d ** -0.5)
    for g, (acc, l) in enumerate(outs):
        o_ref[:, g * d:(g + 1) * d] = (acc * (1.0 / l)).astype(o_ref.dtype).T


def _moba_attention(proj, n_heads, *, group=4):
    s = proj.shape[0]
    t = QUERY_TILE
    assert t % MOBA_BLOCK == 0 and s % t == 0 and n_heads % group == 0
    nb = s // MOBA_BLOCK
    assert nb <= HEAD_DIM
    v_rows = HEAD_DIM + ONES_ROWS
    n_sel = min(MOBA_TOPK, max(nb - 1, 1))
    w = group * HEAD_DIM
    n_groups = n_heads // group
    return pl.pallas_call(
        functools.partial(_moba_kernel, n_sel=n_sel),
        grid=(n_groups, s // t),
        in_specs=[pl.BlockSpec((t, w), lambda h, i: (i, h)),
                  pl.BlockSpec((s, w), lambda h, i: (0, n_groups + h)),
                  pl.BlockSpec((s, w), lambda h, i: (0, 2 * n_groups + h))],
        out_specs=pl.BlockSpec((t, w), lambda h, i: (i, h)),
        out_shape=jax.ShapeDtypeStruct((s, n_heads * HEAD_DIM), bf16),
        scratch_shapes=[pltpu.VMEM((s // KEY_TILE, group * v_rows, KEY_TILE), bf16),
                        pltpu.VMEM((nb, w), f32),
                        pltpu.VMEM((2, group, KEY_TILE, t), f32),
                        pltpu.VMEM((group, v_rows, t), f32)],
        compiler_params=_params(2, INTERLEAVE_FLAGS),
        name="moba_attention",
    )(proj, proj, proj)


def _diff_kernel(q_ref, k_ref, v_ref, lq1_ref, lk1_ref, lq2_ref, lk2_ref, g_ref, o_ref, v_t_ref,
                 t_scr, acc_scr, *, lam_init):
    i = pl.program_id(1)
    tk = KEY_TILE
    d = HEAD_DIM
    group = q_ref.shape[1] // (2 * d)

    v_rows = 2 * d + ONES_ROWS

    @pl.when(i == 0)
    def _():
        _fill_v_t(v_ref, v_t_ref, 2 * d)

    lam = (jnp.exp(jnp.sum(lq1_ref[...] * lk1_ref[...], axis=-1, keepdims=True))
           - jnp.exp(jnp.sum(lq2_ref[...] * lk2_ref[...], axis=-1, keepdims=True))
           + lam_init)
    q_t = _transpose_bf16(q_ref[...])
    streams = []
    for a in range(group):
        vrows = slice(a * v_rows, (a + 1) * v_rows)
        for sub in range(2):
            rows = slice((2 * a + sub) * d, (2 * a + sub + 1) * d)
            streams.append((
                q_t[rows],
                lambda n, diagonal, rows=rows: k_ref[pl.ds(pl.multiple_of(n * tk, tk), tk), rows],
                lambda n, vrows=vrows: v_t_ref[n, vrows, :],
                2 * d))
    outs = _flash_streams(streams, i, t_scr, acc_scr, scale=d ** -0.5)
    for a in range(group):
        (acc1, l1), (acc2, l2) = outs[2 * a], outs[2 * a + 1]
        o = acc1 * (1.0 / l1) - lam * (acc2 * (1.0 / l2))
        ms = jnp.mean(o * o, axis=0, keepdims=True)
        y = (o * lax.rsqrt(ms + SUBLN_EPS)) * g_ref[...]
        o_ref[:, a * 2 * d:(a + 1) * 2 * d] = (y * (1.0 - lam_init)).astype(o_ref.dtype).T


def _diff_attention(proj, n_heads, lq1, lk1, lq2, lk2, g_subln, lam_init, *, group=2):
    s = proj.shape[0]
    t = QUERY_TILE
    assert s % t == 0 and n_heads % group == 0
    d2 = 2 * HEAD_DIM
    w = group * d2
    n_groups = n_heads // group
    vec = pl.BlockSpec((1, HEAD_DIM), lambda h, i: (0, 0))
    return pl.pallas_call(
        functools.partial(_diff_kernel, lam_init=lam_init),
        grid=(n_groups, s // t),
        in_specs=[pl.BlockSpec((t, w), lambda h, i: (i, h)),
                  pl.BlockSpec((s, w), lambda h, i: (0, n_groups + h)),
                  pl.BlockSpec((s, w), lambda h, i: (0, 2 * n_groups + h)),
                  vec, vec, vec, vec,
                  pl.BlockSpec((d2, 1), lambda h, i: (0, 0))],
        out_specs=pl.BlockSpec((t, w), lambda h, i: (i, h)),
        out_shape=jax.ShapeDtypeStruct((s, n_heads * d2), bf16),
        scratch_shapes=[pltpu.VMEM((s // KEY_TILE, group * (d2 + ONES_ROWS), KEY_TILE), bf16),
                        pltpu.VMEM((2, 2 * group, KEY_TILE, t), f32),
                        pltpu.VMEM((2 * group, d2 + ONES_ROWS, t), f32)],
        compiler_params=_params(2, INTERLEAVE_FLAGS),
        name="diff_attention",
    )(proj, proj, proj, lq1.reshape(1, -1), lk1.reshape(1, -1), lq2.reshape(1, -1), lk2.reshape(1, -1),
      g_subln.reshape(d2, 1))


def _mem_attn_kernel(q_ref, mk_ref, mv_ref, g_ref, o_ref, *, n_heads):
    dm = q_ref.shape[1] // n_heads
    scale = dm ** -0.5
    for h in range(n_heads):
        cols = slice(h * dm, (h + 1) * dm)
        kraw = mk_ref[:, cols]
        ms = jnp.mean(kraw * kraw, axis=-1, keepdims=True)
        kn = ((kraw * lax.rsqrt(ms + NORM_EPS)) * g_ref[...]).astype(bf16)
        s = lax.dot_general(q_ref[:, cols], kn, (((1,), (1,)), ((), ())),
                            preferred_element_type=f32) * scale
        m = jnp.max(s, axis=-1, keepdims=True)
        p = jnp.exp(s - m)
        l = jnp.sum(p, axis=-1, keepdims=True)
        o = jnp.dot(p.astype(bf16), mv_ref[:, cols].astype(bf16), preferred_element_type=f32)
        o_ref[:, cols] = (o * (1.0 / l)).astype(o_ref.dtype)


def _mem_attention(proj, mkv, g_knorm, *, tq=1024):
    s = proj.shape[0]
    mlen, wm = mkv.shape[0], mkv.shape[1] // 2
    q_block = proj.shape[1] // wm - 1
    tq = min(tq, s)
    dm = wm // MEM_HEADS
    return pl.pallas_call(
        functools.partial(_mem_attn_kernel, n_heads=MEM_HEADS),
        grid=(s // tq,),
        in_specs=[pl.BlockSpec((tq, wm), lambda i: (i, q_block)),
                  pl.BlockSpec((mlen, wm), lambda i: (0, 0)),
                  pl.BlockSpec((mlen, wm), lambda i: (0, 1)),
                  pl.BlockSpec((1, dm), lambda i: (0, 0))],
        out_specs=pl.BlockSpec((tq, wm), lambda i: (i, 0)),
        out_shape=jax.ShapeDtypeStruct((s, wm), bf16),
        compiler_params=_params(1),
        name="mem_attention",
    )(proj, mkv, mkv, g_knorm.reshape(1, dm))


def kernel(x, mem, positions, g_attn_norm, w_in, w_out, g_qnorm, g_knorm, g_mem_qnorm, g_mem_knorm, g_mem_norm, w_mem_kv, lambda_q1, lambda_k1, lambda_q2, lambda_k2, g_subln, g_ffn_norm, w_gate, w_up, w_down):
    b, s, d_model = x.shape
    assert b == 1
    depth = w_in.shape[0]
    mem_width = w_mem_kv.shape[1] // 2
    self_width = w_out.shape[1] - mem_width
    n_heads = self_width // HEAD_DIM
    mem_head = mem_width // MEM_HEADS

    xs = x.reshape(s, d_model)
    tables = _rope_tables(positions.reshape(s))
    mem_n = _rmsnorm(mem.reshape(mem.shape[1], d_model), g_mem_norm)
    mkv = _matmul(mem_n, w_mem_kv, 0, 2 * mem_width, f32)

    for i in range(depth):
        h = _rmsnorm(xs, g_attn_norm[i])
        proj = _in_proj(h, w_in, i, self_width, mem_width, g_qnorm[i], g_knorm[i], g_mem_qnorm[i],
                        tables)
        if i % 2 == 0:
            self_out = _moba_attention(proj, n_heads)
        else:
            j = i // 2
            lam_init = 0.8 - 0.6 * math.exp(-0.3 * i)
            self_out = _diff_attention(proj, n_heads // 2, lambda_q1[j], lambda_k1[j],
                                       lambda_q2[j], lambda_k2[j], g_subln[j], lam_init)
        mem_out = _mem_attention(proj, mkv, g_mem_knorm[i])
        xs = _outproj(self_out, mem_out, w_out, i, xs)
        f = _rmsnorm(xs, g_ffn_norm[i])
        act = _ffn_up(f, w_gate, w_up, i)
        xs = _ffn_down(act, w_down, i, xs)
    return xs.reshape(b, s, d_model)
```

```python
import functools
import math

import jax
import jax.numpy as jnp
from jax import lax
from jax.experimental import pallas as pl
from jax.experimental.pallas import tpu as pltpu

f32 = jnp.float32
bf16 = jnp.bfloat16

HEAD_DIM = 128
MEM_HEADS = 4
ROT_DIM = HEAD_DIM // 4
ROPE_THETA = 500000.0
MOBA_BLOCK = 256
MOBA_TOPK = 3
NORM_EPS = 1e-6
SUBLN_EPS = 1e-5
LOG2_E = 1.4426950408889634
MASK_VALUE = -1e30

LANES = 128
MXU_COLS = 256
ONES_ROWS = 16
ROW_SPLIT = 4
FFN_UP_SLICE = 512
ATTN_TILE = 256
KEY_TILE = 2 * ATTN_TILE
QUERY_TILE = KEY_TILE
VMEM_LIMIT = 56 * 1024 * 1024
IN_PROJ_VMEM_LIMIT = 60 * 1024 * 1024


def _tile(n, target, align, col0=0):
    t = (min(target, n) // align) * align
    while n % t or col0 % t:
        t -= align
    return t


def _params(n_grid, flags=None, vmem_limit=VMEM_LIMIT):
    return pltpu.CompilerParams(
        dimension_semantics=("arbitrary",) * n_grid, vmem_limit_bytes=vmem_limit, flags=flags)


INTERLEAVE_FLAGS = None


def _rmsnorm_kernel(x_ref, g_ref, o_ref, *, eps):
    x = x_ref[...]
    ms = jnp.mean(x * x, axis=-1, keepdims=True)
    o_ref[...] = ((x * lax.rsqrt(ms + eps)) * g_ref[...]).astype(o_ref.dtype)


def _rmsnorm(x, g, *, eps=NORM_EPS, tm=512):
    m, d = x.shape
    tm = min(tm, m)
    return pl.pallas_call(
        functools.partial(_rmsnorm_kernel, eps=eps),
        grid=(m // tm,),
        in_specs=[pl.BlockSpec((tm, d), lambda i: (i, 0)),
                  pl.BlockSpec((1, d), lambda i: (0, 0))],
        out_specs=pl.BlockSpec((tm, d), lambda i: (i, 0)),
        out_shape=jax.ShapeDtypeStruct((m, d), bf16),
        compiler_params=_params(1),
        name="rmsnorm",
    )(x, g.reshape(1, d))


def _rope_tables_kernel(pos_ref, inv_ref, c_ref, sa_ref, sb_ref):
    ang = pos_ref[...].astype(f32) * inv_ref[...]
    lane = lax.broadcasted_iota(jnp.int32, ang.shape, 1)
    c = jnp.cos(ang)
    s = jnp.sin(ang)
    half = ROT_DIM // 2
    c_ref[...] = jnp.where(lane < ROT_DIM, c, 1.0)
    sa_ref[...] = jnp.where((lane >= half) & (lane < ROT_DIM), s, 0.0)
    sb_ref[...] = jnp.where(lane < half, -s, 0.0)


def _rope_tables(positions, *, tm=1024):
    s = positions.shape[0]
    tm = min(tm, s)
    inv = ROPE_THETA ** (-jnp.arange(0, ROT_DIM, 2, dtype=f32) / ROT_DIM)
    inv_lane = jnp.concatenate(
        [inv, inv, jnp.zeros((HEAD_DIM - ROT_DIM,), f32)]).reshape(1, HEAD_DIM)
    spec = pl.BlockSpec((tm, HEAD_DIM), lambda i: (i, 0))
    shape = jax.ShapeDtypeStruct((s, HEAD_DIM), f32)
    return pl.pallas_call(
        _rope_tables_kernel,
        grid=(s // tm,),
        in_specs=[pl.BlockSpec((tm, 1), lambda i: (i, 0)),
                  pl.BlockSpec((1, HEAD_DIM), lambda i: (0, 0))],
        out_specs=[spec, spec, spec],
        out_shape=[shape, shape, shape],
        compiler_params=_params(1),
        name="rope_tables",
    )(positions.reshape(s, 1), inv_lane)


def _in_proj_kernel(a_ref, w_ref, gq_ref, gk_ref, gm_ref, c_ref, sa_ref, sb_ref, o_ref,
                    *, q_tiles, qk_tiles, v_tiles, mem_head):
    j = pl.program_id(1)
    half = ROT_DIM // 2
    rows = a_ref.shape[0] // ROW_SPLIT

    def tile(head, gain, rope):
        chunk = MXU_COLS if head is None else max(head, MXU_COLS)
        for c0 in range(0, o_ref.shape[1], chunk):
            w = w_ref[:, c0:c0 + chunk].astype(bf16)
            for r0 in range(0, a_ref.shape[0], rows):
                acc = jnp.dot(a_ref[r0:r0 + rows, :], w, preferred_element_type=f32)
                if head is None:
                    o_ref[r0:r0 + rows, c0:c0 + chunk] = acc.astype(o_ref.dtype)
                    continue
                for h0 in range(0, chunk, head):
                    t = acc[:, h0:h0 + head]
                    ms = jnp.mean(t * t, axis=-1, keepdims=True)
                    y = (t * lax.rsqrt(ms + NORM_EPS)) * gain
                    if rope:
                        y = (y * c_ref[r0:r0 + rows, :]
                             + pltpu.roll(y, half, axis=1) * sa_ref[r0:r0 + rows, :]
                             + pltpu.roll(y, head - half, axis=1) * sb_ref[r0:r0 + rows, :])
                    o_ref[r0:r0 + rows, c0 + h0:c0 + h0 + head] = y.astype(o_ref.dtype)

    @pl.when(j < qk_tiles)
    def _():
        tile(HEAD_DIM, jnp.where(j < q_tiles, gq_ref[...], gk_ref[...]), True)

    @pl.when((j >= qk_tiles) & (j < qk_tiles + v_tiles))
    def _():
        tile(None, None, False)

    @pl.when(j >= qk_tiles + v_tiles)
    def _():
        tile(mem_head, gm_ref[...], False)


def _in_proj(a, w, layer, self_width, mem_width, g_q, g_k, g_m, tables, *, tm=1024, tn=1024):
    m, k = a.shape
    n = w.shape[2]
    mem_head = mem_width // MEM_HEADS
    tm = min(tm, m)
    tn = _tile(n, tn, mem_head, math.gcd(self_width, mem_width))
    c, sa, sb = tables
    tab_spec = pl.BlockSpec((tm, HEAD_DIM), lambda i, j: (i, 0))
    g_spec = pl.BlockSpec((1, HEAD_DIM), lambda i, j: (0, 0))
    return pl.pallas_call(
        functools.partial(_in_proj_kernel, q_tiles=self_width // tn, qk_tiles=2 * self_width // tn,
                          v_tiles=self_width // tn, mem_head=mem_head),
        grid=(m // tm, n // tn),
        in_specs=[pl.BlockSpec((tm, k), lambda i, j: (i, 0)),
                  pl.BlockSpec((None, k, tn), lambda i, j: (layer, 0, j)),
                  g_spec, g_spec, pl.BlockSpec((1, mem_head), lambda i, j: (0, 0)),
                  tab_spec, tab_spec, tab_spec],
        out_specs=pl.BlockSpec((tm, tn), lambda i, j: (i, j)),
        out_shape=jax.ShapeDtypeStruct((m, n), bf16),
        compiler_params=_params(2, vmem_limit=IN_PROJ_VMEM_LIMIT),
        name="in_proj",
    )(a, w, g_q.reshape(1, HEAD_DIM), g_k.reshape(1, HEAD_DIM), g_m.reshape(1, mem_head), c, sa, sb)


def _matmul_kernel(a_ref, w_ref, o_ref):
    o_ref[...] = jnp.dot(a_ref[...], w_ref[...].astype(bf16),
                         preferred_element_type=f32).astype(o_ref.dtype)


def _matmul(a, w, col0, n, out_dtype, *, tm=1024, tn=1024):
    m, k = a.shape
    tm = min(tm, m)
    tn = _tile(n, tn, LANES, col0)
    off = col0 // tn
    return pl.pallas_call(
        _matmul_kernel,
        grid=(m // tm, n // tn),
        in_specs=[pl.BlockSpec((tm, k), lambda i, j: (i, 0)),
                  pl.BlockSpec((k, tn), lambda i, j: (0, j + off))],
        out_specs=pl.BlockSpec((tm, tn), lambda i, j: (i, j)),
        out_shape=jax.ShapeDtypeStruct((m, n), out_dtype),
        compiler_params=_params(2),
        name="matmul",
    )(a, w)


def _outproj_kernel(a1_ref, a2_ref, w1_ref, w2_ref, r_ref, o_ref):
    acc = jnp.dot(a1_ref[...], w1_ref[...].astype(bf16), preferred_element_type=f32)
    acc = acc + jnp.dot(a2_ref[...], w2_ref[...].astype(bf16), preferred_element_type=f32)
    o_ref[...] = r_ref[...] + acc


def _outproj(a1, a2, w, layer, resid, *, tm=1024, tn=512):
    m, k1 = a1.shape
    k2 = a2.shape[1]
    n = w.shape[2]
    tm = min(tm, m)
    tn = _tile(n, tn, LANES)
    assert k1 % k2 == 0
    return pl.pallas_call(
        _outproj_kernel,
        grid=(m // tm, n // tn),
        in_specs=[pl.BlockSpec((tm, k1), lambda i, j: (i, 0)),
                  pl.BlockSpec((tm, k2), lambda i, j: (i, 0)),
                  pl.BlockSpec((None, k1, tn), lambda i, j: (layer, 0, j)),
                  pl.BlockSpec((None, k2, tn), lambda i, j: (layer, k1 // k2, j)),
                  pl.BlockSpec((tm, tn), lambda i, j: (i, j))],
        out_specs=pl.BlockSpec((tm, tn), lambda i, j: (i, j)),
        out_shape=jax.ShapeDtypeStruct((m, n), f32),
        compiler_params=_params(2),
        name="outproj",
    )(a1, a2, w, w, resid)


def _ffn_up_kernel(a_ref, wg_ref, wu_ref, o_ref):
    wg = wg_ref[...].astype(bf16)
    wu = wu_ref[...].astype(bf16)
    rows = min(FFN_UP_SLICE, a_ref.shape[0])
    for r0 in range(0, a_ref.shape[0], rows):
        a = a_ref[r0:r0 + rows, :]
        g = jnp.dot(a, wg, preferred_element_type=f32)
        u = jnp.dot(a, wu, preferred_element_type=f32)
        o_ref[r0:r0 + rows, :] = ((g * jax.nn.sigmoid(g)) * u).astype(o_ref.dtype)


def _ffn_up(a, wg, wu, layer, *, tm=4096, tn=256):
    m, k = a.shape
    n = wg.shape[2]
    tm = min(tm, m)
    tn = _tile(n, tn, LANES)
    w_spec = pl.BlockSpec((None, k, tn), lambda i, j: (layer, 0, j))
    return pl.pallas_call(
        _ffn_up_kernel,
        grid=(m // tm, n // tn),
        in_specs=[pl.BlockSpec((tm, k), lambda i, j: (i, 0), pipeline_mode=pl.Buffered(1)),
                  w_spec, w_spec],
        out_specs=pl.BlockSpec((tm, tn), lambda i, j: (i, j)),
        out_shape=jax.ShapeDtypeStruct((m, n), bf16),
        compiler_params=_params(2),
        name="ffn_up",
    )(a, wg, wu)


def _ffn_down_kernel(a_ref, w_ref, r_ref, o_ref):
    o_ref[...] = r_ref[...] + jnp.dot(a_ref[...], w_ref[...].astype(bf16),
                                      preferred_element_type=f32)


def _ffn_down(a, w, layer, resid, *, tm=1024, tn=256):
    m, k = a.shape
    n = w.shape[2]
    tm = min(tm, m)
    tn = _tile(n, tn, LANES)
    return pl.pallas_call(
        _ffn_down_kernel,
        grid=(m // tm, n // tn),
        in_specs=[pl.BlockSpec((tm, k), lambda i, j: (i, 0), pipeline_mode=pl.Buffered(1)),
                  pl.BlockSpec((None, k, tn), lambda i, j: (layer, 0, j)),
                  pl.BlockSpec((tm, tn), lambda i, j: (i, j))],
        out_specs=pl.BlockSpec((tm, tn), lambda i, j: (i, j)),
        out_shape=jax.ShapeDtypeStruct((m, n), f32),
        compiler_params=_params(2),
        name="ffn_down",
    )(a, w, resid)


def _transpose_bf16(x):
    return x.T


def _flash_streams(streams, i, t_scr, acc_scr, *, scale):
    t, tk = QUERY_TILE, KEY_TILE
    assert t == tk
    c = scale * LOG2_E
    n_streams = len(streams)
    n_past = i
    key = lax.broadcasted_iota(jnp.int32, (tk, t), 0)
    qry = lax.broadcasted_iota(jnp.int32, (tk, t), 1)
    causal = key <= qry

    def stage_scores(n, slot, diagonal):
        raws = [jnp.dot(load_k(n, diagonal), q_rhs, preferred_element_type=f32)
                for q_rhs, load_k, _, _ in streams]
        maxima = []
        for idx, raw in enumerate(raws):
            s = raw * c
            if diagonal:
                s = jnp.where(causal, s, MASK_VALUE)
            t_scr[slot, idx] = s
            maxima.append(jnp.max(s, axis=0, keepdims=True))
        return maxima

    def accumulate(m_old, maxima, slot, v_pair):
        m_out = []
        for idx, (_, _, load_v_t, _) in enumerate(streams):
            m_new = jnp.maximum(m_old[idx], maxima[idx])
            alpha = jnp.exp2(m_old[idx] - m_new)
            p = jnp.exp2(t_scr[slot, idx] - m_new)
            pv = jnp.dot(load_v_t(v_pair), p.astype(bf16), preferred_element_type=f32)
            acc_scr[idx] = alpha * acc_scr[idx] + pv
            m_out.append(m_new)
        return m_out

    maxima = stage_scores(n_past, 0, True)
    m_run = [jnp.full((1, t), -jnp.inf, f32) for _ in streams]
    for idx in range(n_streams):
        acc_scr[idx] = jnp.zeros(acc_scr.shape[1:], f32)

    def step(n, slot, m_run, maxima):
        maxima_next = stage_scores(n, 1 - slot, False)
        m_run = accumulate(m_run, maxima, slot, jnp.where(n == 0, n_past, n - 1))
        return m_run, maxima_next

    def body(trip, carry):
        m_run, maxima = list(carry[:n_streams]), list(carry[n_streams:])
        m_run, maxima = step(2 * trip, 0, m_run, maxima)
        m_run, maxima = step(2 * trip + 1, 1, m_run, maxima)
        return tuple(m_run + maxima)

    carry = lax.fori_loop(0, n_past // 2, body, tuple(m_run + maxima))
    m_run, maxima = list(carry[:n_streams]), list(carry[n_streams:])

    @pl.when(n_past % 2 == 1)
    def _():
        m_mid, maxima_last = step(n_past - 1, 0, m_run, maxima)
        accumulate(m_mid, maxima_last, 1, n_past - 1)

    @pl.when(n_past % 2 == 0)
    def _():
        accumulate(m_run, maxima, 0, jnp.maximum(n_past - 1, 0))

    return [(acc_scr[idx, :dv, :], acc_scr[idx, dv:dv + 1, :])
            for idx, (_, _, _, dv) in enumerate(streams)]


def _fill_v_t(v_ref, v_t_ref, dv):
    t = KEY_TILE
    rows = dv + ONES_ROWS
    for n in range(v_t_ref.shape[0]):
        v_t = _transpose_bf16(v_ref[n * t:(n + 1) * t, :])
        for h in range(v_ref.shape[1] // dv):
            v_t_ref[n, h * rows:h * rows + dv, :] = v_t[h * dv:(h + 1) * dv]
            v_t_ref[n, h * rows + dv:(h + 1) * rows, :] = jnp.ones((ONES_ROWS, t), bf16)


def _moba_kernel(q_ref, k_ref, v_ref, o_ref, v_t_ref, kmean_ref, t_scr, acc_scr, *, n_sel):
    i = pl.program_id(1)
    t, tk, blk_len = QUERY_TILE, KEY_TILE, MOBA_BLOCK
    d = HEAD_DIM
    nb = kmean_ref.shape[0]
    group = q_ref.shape[1] // d
    v_rows = d + ONES_ROWS

    @pl.when(i == 0)
    def _():
        _fill_v_t(v_ref, v_t_ref, d)
        for n in range(nb):
            kb = k_ref[n * blk_len:(n + 1) * blk_len, :].astype(f32)
            kmean_ref[n:n + 1, :] = jnp.sum(kb, axis=0, keepdims=True) / float(blk_len)

    q_t = _transpose_bf16(q_ref[...])
    blk = lax.broadcasted_iota(jnp.int32, (nb, t), 0)
    own = (tk // blk_len) * i + lax.broadcasted_iota(jnp.int32, (nb, t), 1) // blk_len
    past = blk < own
    lane = lax.broadcasted_iota(jnp.int32, (tk, d), 1)
    sub_block = lax.broadcasted_iota(jnp.int32, (tk, d), 0) // blk_len

    def block_onehot(n):
        return jnp.where(lane == (tk // blk_len) * n + sub_block, 1.0, 0.0).astype(bf16)

    streams = []
    for g in range(group):
        rows = slice(g * d, (g + 1) * d)
        gate = jnp.dot(kmean_ref[:, rows].astype(bf16), q_t[rows], preferred_element_type=f32)
        gt = jnp.where(past, gate, -jnp.inf)
        sel = jnp.zeros(gate.shape, jnp.bool_)
        for _ in range(n_sel):
            mx = jnp.max(gt, axis=0, keepdims=True)
            idx = jnp.min(jnp.where(gt == mx, blk, nb), axis=0, keepdims=True)
            pick = blk == idx
            sel = sel | pick
            gt = jnp.where(pick, -jnp.inf, gt)
        mask_rows = jnp.concatenate(
            [jnp.where((sel & past) | (blk == own), 0.0, MASK_VALUE), jnp.zeros((d - nb, t), f32)],
            axis=0).astype(bf16)
        streams.append((
            jnp.concatenate([q_t[rows], mask_rows], axis=0),
            lambda n, diagonal, rows=rows: jnp.concatenate(
                [k_ref[pl.ds(pl.multiple_of(n * tk, tk), tk), rows], block_onehot(n)], axis=1),
            lambda n, g=g: v_t_ref[n, g * v_rows:(g + 1) * v_rows, :],
            d))

    outs = _flash_streams(streams, i, t_scr, acc_scr, scale=d ** -0.5)
    for g, (acc, l) in enumerate(outs):
        o_ref[:, g * d:(g + 1) * d] = (acc * (1.0 / l)).astype(o_ref.dtype).T


def _moba_attention(proj, n_heads, *, group=4):
    s = proj.shape[0]
    t = QUERY_TILE
    assert t % MOBA_BLOCK == 0 and s % t == 0 and n_heads % group == 0
    nb = s // MOBA_BLOCK
    assert nb <= HEAD_DIM
    v_rows = HEAD_DIM + ONES_ROWS
    n_sel = min(MOBA_TOPK, max(nb - 1, 1))
    w = group * HEAD_DIM
    n_groups = n_heads // group
    return pl.pallas_call(
        functools.partial(_moba_kernel, n_sel=n_sel),
        grid=(n_groups, s // t),
        in_specs=[pl.BlockSpec((t, w), lambda h, i: (i, h)),
                  pl.BlockSpec((s, w), lambda h, i: (0, n_groups + h)),
                  pl.BlockSpec((s, w), lambda h, i: (0, 2 * n_groups + h))],
        out_specs=pl.BlockSpec((t, w), lambda h, i: (i, h)),
        out_shape=jax.ShapeDtypeStruct((s, n_heads * HEAD_DIM), bf16),
        scratch_shapes=[pltpu.VMEM((s // KEY_TILE, group * v_rows, KEY_TILE), bf16),
                        pltpu.VMEM((nb, w), f32),
                        pltpu.VMEM((2, group, KEY_TILE, t), f32),
                        pltpu.VMEM((group, v_rows, t), f32)],
        compiler_params=_params(2, INTERLEAVE_FLAGS),
        name="moba_attention",
    )(proj, proj, proj)


def _diff_kernel(q_ref, k_ref, v_ref, lq1_ref, lk1_ref, lq2_ref, lk2_ref, g_ref, o_ref, v_t_ref,
                 t_scr, acc_scr, *, lam_init):
    i = pl.program_id(1)
    tk = KEY_TILE
    d = HEAD_DIM
    group = q_ref.shape[1] // (2 * d)

    v_rows = 2 * d + ONES_ROWS

    @pl.when(i == 0)
    def _():
        _fill_v_t(v_ref, v_t_ref, 2 * d)

    lam = (jnp.exp(jnp.sum(lq1_ref[...] * lk1_ref[...], axis=-1, keepdims=True))
           - jnp.exp(jnp.sum(lq2_ref[...] * lk2_ref[...], axis=-1, keepdims=True))
           + lam_init)
    q_t = _transpose_bf16(q_ref[...])
    streams = []
    for a in range(group):
        vrows = slice(a * v_rows, (a + 1) * v_rows)
        for sub in range(2):
            rows = slice((2 * a + sub) * d, (2 * a + sub + 1) * d)
            streams.append((
                q_t[rows],
                lambda n, diagonal, rows=rows: k_ref[pl.ds(pl.multiple_of(n * tk, tk), tk), rows],
                lambda n, vrows=vrows: v_t_ref[n, vrows, :],
                2 * d))
    outs = _flash_streams(streams, i, t_scr, acc_scr, scale=d ** -0.5)
    for a in range(group):
        (acc1, l1), (acc2, l2) = outs[2 * a], outs[2 * a + 1]
        o = acc1 * (1.0 / l1) - lam * (acc2 * (1.0 / l2))
        ms = jnp.mean(o * o, axis=0, keepdims=True)
        y = (o * lax.rsqrt(ms + SUBLN_EPS)) * g_ref[...]
        o_ref[:, a * 2 * d:(a + 1) * 2 * d] = (y * (1.0 - lam_init)).astype(o_ref.dtype).T


def _diff_attention(proj, n_heads, lq1, lk1, lq2, lk2, g_subln, lam_init, *, group=2):
    s = proj.shape[0]
    t = QUERY_TILE
    assert s % t == 0 and n_heads % group == 0
    d2 = 2 * HEAD_DIM
    w = group * d2
    n_groups = n_heads // group
    vec = pl.BlockSpec((1, HEAD_DIM), lambda h, i: (0, 0))
    return pl.pallas_call(
        functools.partial(_diff_kernel, lam_init=lam_init),
        grid=(n_groups, s // t),
        in_specs=[pl.BlockSpec((t, w), lambda h, i: (i, h)),
                  pl.BlockSpec((s, w), lambda h, i: (0, n_groups + h)),
                  pl.BlockSpec((s, w), lambda h, i: (0, 2 * n_groups + h)),
                  vec, vec, vec, vec,
                  pl.BlockSpec((d2, 1), lambda h, i: (0, 0))],
        out_specs=pl.BlockSpec((t, w), lambda h, i: (i, h)),
        out_shape=jax.ShapeDtypeStruct((s, n_heads * d2), bf16),
        scratch_shapes=[pltpu.VMEM((s // KEY_TILE, group * (d2 + ONES_ROWS), KEY_TILE), bf16),
                        pltpu.VMEM((2, 2 * group, KEY_TILE, t), f32),
                        pltpu.VMEM((2 * group, d2 + ONES_ROWS, t), f32)],
        compiler_params=_params(2, INTERLEAVE_FLAGS),
        name="diff_attention",
    )(proj, proj, proj, lq1.reshape(1, -1), lk1.reshape(1, -1), lq2.reshape(1, -1), lk2.reshape(1, -1),
      g_subln.reshape(d2, 1))


def _mem_attn_kernel(q_ref, mk_ref, mv_ref, g_ref, o_ref, *, n_heads):
    dm = q_ref.shape[1] // n_heads
    scale = dm ** -0.5
    for h in range(n_heads):
        cols = slice(h * dm, (h + 1) * dm)
        kraw = mk_ref[:, cols]
        ms = jnp.mean(kraw * kraw, axis=-1, keepdims=True)
        kn = ((kraw * lax.rsqrt(ms + NORM_EPS)) * g_ref[...]).astype(bf16)
        s = lax.dot_general(q_ref[:, cols], kn, (((1,), (1,)), ((), ())),
                            preferred_element_type=f32) * scale
        m = jnp.max(s, axis=-1, keepdims=True)
        p = jnp.exp(s - m)
        l = jnp.sum(p, axis=-1, keepdims=True)
        o = jnp.dot(p.astype(bf16), mv_ref[:, cols].astype(bf16), preferred_element_type=f32)
        o_ref[:, cols] = (o * (1.0 / l)).astype(o_ref.dtype)


def _mem_attention(proj, mkv, g_knorm, *, tq=1024):
    s = proj.shape[0]
    mlen, wm = mkv.shape[0], mkv.shape[1] // 2
    q_block = proj.shape[1] // wm - 1
    tq = min(tq, s)
    dm = wm // MEM_HEADS
    return pl.pallas_call(
        functools.partial(_mem_attn_kernel, n_heads=MEM_HEADS),
        grid=(s // tq,),
        in_specs=[pl.BlockSpec((tq, wm), lambda i: (i, q_block)),
                  pl.BlockSpec((mlen, wm), lambda i: (0, 0)),
                  pl.BlockSpec((mlen, wm), lambda i: (0, 1)),
                  pl.BlockSpec((1, dm), lambda i: (0, 0))],
        out_specs=pl.BlockSpec((tq, wm), lambda i: (i, 0)),
        out_shape=jax.ShapeDtypeStruct((s, wm), bf16),
        compiler_params=_params(1),
        name="mem_attention",
    )(proj, mkv, mkv, g_knorm.reshape(1, dm))


def kernel(x, mem, positions, g_attn_norm, w_in, w_out, g_qnorm, g_knorm, g_mem_qnorm, g_mem_knorm, g_mem_norm, w_mem_kv, lambda_q1, lambda_k1, lambda_q2, lambda_k2, g_subln, g_ffn_norm, w_gate, w_up, w_down):
    b, s, d_model = x.shape
    assert b == 1
    depth = w_in.shape[0]
    mem_width = w_mem_kv.shape[1] // 2
    self_width = w_out.shape[1] - mem_width
    n_heads = self_width // HEAD_DIM
    mem_head = mem_width // MEM_HEADS

    xs = x.reshape(s, d_model)
    tables = _rope_tables(positions.reshape(s))
    mem_n = _rmsnorm(mem.reshape(mem.shape[1], d_model), g_mem_norm)
    mkv = _matmul(mem_n, w_mem_kv, 0, 2 * mem_width, f32)

    for i in range(depth):
        h = _rmsnorm(xs, g_attn_norm[i])
        proj = _in_proj(h, w_in, i, self_width, mem_width, g_qnorm[i], g_knorm[i], g_mem_qnorm[i],
                        tables)
        if i % 2 == 0:
            self_out = _moba_attention(proj, n_heads)
        else:
            j = i // 2
            lam_init = 0.8 - 0.6 * math.exp(-0.3 * i)
            self_out = _diff_attention(proj, n_heads // 2, lambda_q1[j], lambda_k1[j],
                                       lambda_q2[j], lambda_k2[j], g_subln[j], lam_init)
        mem_out = _mem_attention(proj, mkv, g_mem_knorm[i])
        xs = _outproj(self_out, mem_out, w_out, i, xs)
        f = _rmsnorm(xs, g_ffn_norm[i])
        act = _ffn_up(f, w_gate, w_up, i)
        xs = _ffn_down(act, w_down, i, xs)
    return xs.reshape(b, s, d_model)
```
